```python
import math
import jax, jax.numpy as jnp
from jax import lax
import numpy as np

D_MODEL = 2048
BATCH = 2
SEQ = 4096
DEPTH = 4

MIX_WIDTH = D_MODEL // 2
N_BRANCHES = 3
EPS = 1e-6
CHUNK = 128
GM_WIDTH = MIX_WIDTH
GM_GROUP_CH = 128
GM_GROUPS = GM_WIDTH // GM_GROUP_CH
S5_WIDTH = MIX_WIDTH
S5_GROUP_CH = 16
S5_GROUPS = S5_WIDTH // S5_GROUP_CH
S5_STATE = 64
DT_MIN = 1e-3
DT_MAX = 1e-1
SB_WIDTH = MIX_WIDTH
SB_HEAD_DIM = 128
SB_HEADS = SB_WIDTH // SB_HEAD_DIM
SB_BLOCK = 128
PROJ_COLS = 2 * GM_WIDTH + S5_WIDTH + 3 * SB_WIDTH + N_BRANCHES * D_MODEL
D_FF = 4 * D_MODEL

kernel_name = "hybrid_gated_gmlp_s5_stickbreak"


def rmsnorm(x, g):
    xf = x.astype(jnp.float32)
    y = xf * lax.rsqrt(jnp.mean(xf * xf, axis=-1, keepdims=True) + EPS)
    return (y * g.astype(jnp.float32)).astype(x.dtype)


def gmlp_mixer(uv, norm_g, w_s, b_s):
    bsz, seq, _ = uv.shape
    z = jax.nn.gelu(uv)
    u, v = jnp.split(z, 2, axis=-1)
    v = rmsnorm(v, norm_g)
    v = v.reshape(bsz, seq // CHUNK, CHUNK, GM_GROUPS, GM_GROUP_CH)
    w = jnp.tril(w_s)
    mixed = jnp.einsum('gts,bcsgh->bctgh', w, v) + jnp.transpose(b_s)[None, None, :, :, None]
    return u * mixed.reshape(bsz, seq, GM_WIDTH)


def s5_mixer(xin, lam_re, lam_im, log_dt, b_re, b_im, c_re, c_im, d, w_glu, b_glu):
    f32 = jnp.float32
    bsz, seq, _ = xin.shape
    u = xin.astype(f32).reshape(bsz, seq, S5_GROUPS, S5_GROUP_CH)
    lam_re = lam_re.astype(f32)
    lam_im = lam_im.astype(f32)
    b_re = b_re.astype(f32)
    b_im = b_im.astype(f32)
    dt = jnp.exp(log_dt.astype(f32))[:, None]
    mag = jnp.exp(lam_re * dt)
    ab_re = mag * jnp.cos(lam_im * dt)
    ab_im = mag * jnp.sin(lam_im * dt)
    den = lam_re * lam_re + lam_im * lam_im
    n_re = ab_re - 1.0
    n_im = ab_im
    k_re = (n_re * lam_re + n_im * lam_im) / den
    k_im = (n_im * lam_re - n_re * lam_im) / den
    bb_re = k_re[..., None] * b_re - k_im[..., None] * b_im
    bb_im = k_re[..., None] * b_im + k_im[..., None] * b_re
    bu_re = jnp.einsum('blgh,gph->blgp', u, bb_re)
    bu_im = jnp.einsum('blgh,gph->blgp', u, bb_im)
    a_re = jnp.broadcast_to(ab_re, bu_re.shape)
    a_im = jnp.broadcast_to(ab_im, bu_im.shape)

    def combine(e_i, e_j):
        ar_i, ai_i, br_i, bi_i = e_i
        ar_j, ai_j, br_j, bi_j = e_j
        return (ar_j * ar_i - ai_j * ai_i,
                ar_j * ai_i + ai_j * ar_i,
                ar_j * br_i - ai_j * bi_i + br_j,
                ar_j * bi_i + ai_j * br_i + bi_j)

    _, _, s_re, s_im = lax.associative_scan(combine, (a_re, a_im, bu_re, bu_im), axis=1)
    y = (jnp.einsum('blgp,ghp->blgh', s_re, c_re.astype(f32))
         - jnp.einsum('blgp,ghp->blgh', s_im, c_im.astype(f32)))
    y = y.reshape(bsz, seq, S5_WIDTH) + d.astype(f32) * xin.astype(f32)
    g = jax.nn.gelu(y)
    out = g * jax.nn.sigmoid(g @ w_glu.astype(f32) + b_glu.astype(f32))
    return out.astype(xin.dtype)


def stick_breaking_attention(q, k, v):
    f32 = jnp.float32
    seq = q.shape[2]
    scale = SB_HEAD_DIM ** -0.5
    outs = []
    for blk in range(seq // SB_BLOCK):
        t0 = blk * SB_BLOCK
        t1 = t0 + SB_BLOCK
        qb = q[:, :, t0:t1].astype(f32)
        kb = k[:, :, :t1].astype(f32)
        vb = v[:, :, :t1].astype(f32)
        z = jnp.einsum('bhtd,bhsd->bhts', qb, kb) * scale
        mask = jnp.arange(t1)[None, :] < jnp.arange(t0, t1)[:, None]
        log_1m = jnp.where(mask, jax.nn.log_sigmoid(-z), 0.0)
        after = lax.cumsum(log_1m, axis=3, reverse=True) - log_1m
        w = jnp.where(mask, jnp.exp(jax.nn.log_sigmoid(z) + after), 0.0)
        outs.append(jnp.einsum('bhts,bhsd->bhtd', w, vb))
    return jnp.concatenate(outs, axis=2).astype(q.dtype)


def setup_inputs(seed: int = 0) -> dict:
    key = jax.random.key(seed)
    ks = jax.random.split(key, 32)

    def nrm(k, shape, scale):
        return jax.random.normal(k, shape, jnp.float32) * scale

    x = nrm(ks[0], (BATCH, SEQ, D_MODEL), 1.0)
    norm1_g = 1.0 + nrm(ks[1], (DEPTH, D_MODEL), 0.05)
    w_in = nrm(ks[2], (DEPTH, D_MODEL, PROJ_COLS), D_MODEL ** -0.5)
    b_gate = nrm(ks[3], (DEPTH, N_BRANCHES, D_MODEL), 0.1)
    gm_norm_g = 1.0 + nrm(ks[4], (DEPTH, GM_WIDTH), 0.05)
    gm_w_s = nrm(ks[5], (DEPTH, GM_GROUPS, CHUNK, CHUNK), CHUNK ** -0.5)
    gm_b_s = 1.0 + nrm(ks[6], (DEPTH, GM_GROUPS, CHUNK), 0.1)
    s5_lambda_re = -0.5 + nrm(ks[7], (DEPTH, S5_GROUPS, S5_STATE), 0.01)
    s5_lambda_im = (math.pi * jnp.arange(S5_STATE, dtype=jnp.float32))[None, None, :] \
        + nrm(ks[8], (DEPTH, S5_GROUPS, S5_STATE), 0.01)
    s5_log_dt = jax.random.uniform(ks[9], (DEPTH, S5_GROUPS), jnp.float32,
                                   math.log(DT_MIN), math.log(DT_MAX))
    s5_b_re = nrm(ks[10], (DEPTH, S5_GROUPS, S5_STATE, S5_GROUP_CH), (2 * S5_GROUP_CH) ** -0.5)
    s5_b_im = nrm(ks[11], (DEPTH, S5_GROUPS, S5_STATE, S5_GROUP_CH), (2 * S5_GROUP_CH) ** -0.5)
    s5_c_re = nrm(ks[12], (DEPTH, S5_GROUPS, S5_GROUP_CH, S5_STATE), (2 * S5_STATE) ** -0.5)
    s5_c_im = nrm(ks[13], (DEPTH, S5_GROUPS, S5_GROUP_CH, S5_STATE), (2 * S5_STATE) ** -0.5)
    s5_d = nrm(ks[14], (DEPTH, S5_WIDTH), 1.0)
    s5_w_glu = nrm(ks[15], (DEPTH, S5_WIDTH, S5_WIDTH), S5_WIDTH ** -0.5)
    s5_b_glu = nrm(ks[16], (DEPTH, S5_WIDTH), 0.02)
    w_branch = nrm(ks[17], (DEPTH, N_BRANCHES, MIX_WIDTH, D_MODEL), MIX_WIDTH ** -0.5)
    w_out = nrm(ks[18], (DEPTH, D_MODEL, D_MODEL), D_MODEL ** -0.5)
    norm2_g = 1.0 + nrm(ks[19], (DEPTH, D_MODEL), 0.05)
    w_mlp_in = nrm(ks[20], (DEPTH, D_MODEL, D_FF), D_MODEL ** -0.5)
    w_mlp_out = nrm(ks[21], (DEPTH, D_FF, D_MODEL), D_FF ** -0.5)
    final_g = 1.0 + nrm(ks[22], (D_MODEL,), 0.05)
    return {"x": x, "norm1_g": norm1_g, "w_in": w_in, "b_gate": b_gate,
            "gm_norm_g": gm_norm_g, "gm_w_s": gm_w_s, "gm_b_s": gm_b_s,
            "s5_lambda_re": s5_lambda_re, "s5_lambda_im": s5_lambda_im, "s5_log_dt": s5_log_dt,
            "s5_b_re": s5_b_re, "s5_b_im": s5_b_im, "s5_c_re": s5_c_re, "s5_c_im": s5_c_im,
            "s5_d": s5_d, "s5_w_glu": s5_w_glu, "s5_b_glu": s5_b_glu,
            "w_branch": w_branch, "w_out": w_out, "norm2_g": norm2_g,
            "w_mlp_in": w_mlp_in, "w_mlp_out": w_mlp_out, "final_g": final_g}


def reference(x, norm1_g, w_in, b_gate, gm_norm_g, gm_w_s, gm_b_s,
              s5_lambda_re, s5_lambda_im, s5_log_dt, s5_b_re, s5_b_im, s5_c_re, s5_c_im,
              s5_d, s5_w_glu, s5_b_glu, w_branch, w_out, norm2_g,
              w_mlp_in, w_mlp_out, final_g):
    bsz, seq, _ = x.shape
    o_a = 2 * GM_WIDTH
    o_b = o_a + S5_WIDTH
    o_c = o_b + 3 * SB_WIDTH
    for l in range(DEPTH):
        h = rmsnorm(x, norm1_g[l])
        proj = h @ w_in[l]
        uv = proj[..., :o_a]
        s5_in = proj[..., o_a:o_b]
        qkv = proj[..., o_b:o_c].reshape(bsz, seq, 3, SB_HEADS, SB_HEAD_DIM)
        qkv = jnp.transpose(qkv, (2, 0, 3, 1, 4))
        gates = jax.nn.sigmoid(proj[..., o_c:].reshape(bsz, seq, N_BRANCHES, D_MODEL) + b_gate[l])

        y_a = gmlp_mixer(uv, gm_norm_g[l], gm_w_s[l], gm_b_s[l])
        y_b = s5_mixer(s5_in, s5_lambda_re[l], s5_lambda_im[l], s5_log_dt[l],
                       s5_b_re[l], s5_b_im[l], s5_c_re[l], s5_c_im[l],
                       s5_d[l], s5_w_glu[l], s5_b_glu[l])
        y_c = stick_breaking_attention(qkv[0], qkv[1], qkv[2])
        y_c = jnp.transpose(y_c, (0, 2, 1, 3)).reshape(bsz, seq, SB_WIDTH)

        ys = jnp.stack([y_a, y_b, y_c], axis=2)
        br = jnp.einsum('blnw,nwd->blnd', ys, w_branch[l])
        merged = jnp.sum(gates * br, axis=2)
        x = x + merged @ w_out[l]
        h2 = rmsnorm(x, norm2_g[l])
        x = x + jnp.square(jax.nn.relu(h2 @ w_mlp_in[l])) @ w_mlp_out[l]
    return rmsnorm(x, final_g)
```

```python
import functools
import math

import jax
import jax.numpy as jnp
from jax import lax
from jax.experimental import pallas as pl
from jax.experimental.pallas import tpu as pltpu

F32 = jnp.float32
BF16 = jnp.bfloat16

EPS = 1e-6
LANES = 128
SUBLANES = 8
VMEM_LIMIT_BYTES = 56 * 1024 * 1024

GM_CHUNK = 128
GM_GROUP_CH = 128
S5_GROUP_CH = 16
S5_STATE = 64
S5_BUNDLE_GROUPS = 8
S5_BUNDLE_CH = S5_BUNDLE_GROUPS * S5_GROUP_CH
S5_BUNDLE_STATE = S5_BUNDLE_GROUPS * S5_STATE
S5_SEGMENTS = SUBLANES
SB_HEAD_DIM = 128


def _params(*semantics):
    return pltpu.CompilerParams(dimension_semantics=semantics, vmem_limit_bytes=VMEM_LIMIT_BYTES)


def _rmsnorm_kernel(x_ref, g_ref, o_ref):
    x = x_ref[...]
    y = x * lax.rsqrt(jnp.mean(x * x, axis=-1, keepdims=True) + EPS)
    o_ref[...] = (y * g_ref[...]).astype(o_ref.dtype)


def rmsnorm(x, g, out_dtype, tm=512):
    n, d = x.shape
    tm = min(tm, n)
    return pl.pallas_call(
        _rmsnorm_kernel,
        grid=(n // tm,),
        in_specs=[pl.BlockSpec((tm, d), lambda i: (i, 0)),
                  pl.BlockSpec((1, d), lambda i: (0, 0))],
        out_specs=pl.BlockSpec((tm, d), lambda i: (i, 0)),
        out_shape=jax.ShapeDtypeStruct((n, d), out_dtype),
        compiler_params=_params("parallel"),
        name="rmsnorm",
    )(x, g.reshape(1, d).astype(F32))


def _matmul_kernel(*refs, nk, act, has_bias, has_res):
    x_ref, w_ref = refs[0], refs[1]
    pos = 2
    b_ref = r_ref = None
    if has_bias:
        b_ref = refs[pos]
        pos += 1
    if has_res:
        r_ref = refs[pos]
        pos += 1
    o_ref = refs[pos]
    acc_ref = refs[pos + 1] if nk > 1 else None

    def finish(acc):
        if has_bias:
            acc = acc + b_ref[...]
        if act == "sigmoid":
            acc = jax.nn.sigmoid(acc)
        elif act == "relu2":
            acc = jnp.square(jnp.maximum(acc, 0.0))
        if has_res:
            acc = acc + r_ref[...]
        o_ref[...] = acc.astype(o_ref.dtype)

    part = jnp.dot(x_ref[...], w_ref[...], preferred_element_type=F32)
    if nk == 1:
        finish(part)
        return
    k = pl.program_id(2)

    @pl.when(k == 0)
    def _():
        acc_ref[...] = part

    @pl.when(k > 0)
    def _():
        acc_ref[...] += part

    @pl.when(k == nk - 1)
    def _():
        finish(acc_ref[...])


def matmul(x, w, *, bias=None, act=None, residual=None, out_dtype=BF16, tm=1024, tn=1024, tk=2048):
    m, kdim = x.shape
    n = w.shape[1]
    tm, tn, tk = min(tm, m), min(tn, n), min(tk, kdim)
    nk = kdim // tk
    in_specs = [pl.BlockSpec((tm, tk), lambda i, j, k: (i, k)),
                pl.BlockSpec((tk, tn), lambda i, j, k: (k, j))]
    args = [x, w]
    if bias is not None:
        in_specs.append(pl.BlockSpec((1, tn), lambda i, j, k: (0, j)))
        args.append(bias.reshape(1, n).astype(F32))
    if residual is not None:
        in_specs.append(pl.BlockSpec((tm, tn), lambda i, j, k: (i, j)))
        args.append(residual)
    kern = functools.partial(_matmul_kernel, nk=nk, act=act, has_bias=bias is not None,
                             has_res=residual is not None)
    return pl.pallas_call(
        kern,
        grid=(m // tm, n // tn, nk),
        in_specs=in_specs,
        out_specs=pl.BlockSpec((tm, tn), lambda i, j, k: (i, j)),
        out_shape=jax.ShapeDtypeStruct((m, n), out_dtype),
        scratch_shapes=[pltpu.VMEM((tm, tn), F32)] if nk > 1 else [],
        compiler_params=_params("parallel", "parallel", "arbitrary"),
        name="matmul",
    )(*args)


def _gmlp_kernel(uv_ref, g_ref, w_ref, bt_ref, o_ref, *, width):
    z = jax.nn.gelu(uv_ref[...].astype(F32))
    u = z[:, :width]
    v = z[:, width:]
    v = v * lax.rsqrt(jnp.mean(v * v, axis=-1, keepdims=True) + EPS) * g_ref[...]
    v = v.astype(BF16)
    tm = u.shape[0]
    t_idx = lax.broadcasted_iota(jnp.int32, (GM_CHUNK, GM_CHUNK), 0)
    s_idx = lax.broadcasted_iota(jnp.int32, (GM_CHUNK, GM_CHUNK), 1)
    causal = s_idx <= t_idx
    for g in range(width // GM_GROUP_CH):
        w = jnp.where(causal, w_ref[g], 0.0).astype(BF16)
        bias = bt_ref[:, g:g + 1]
        cols = slice(g * GM_GROUP_CH, (g + 1) * GM_GROUP_CH)
        for c in range(tm // GM_CHUNK):
            rows = slice(c * GM_CHUNK, (c + 1) * GM_CHUNK)
            mixed = jnp.dot(w, v[rows, cols], preferred_element_type=F32) + bias
            o_ref[rows, cols] = (u[rows, cols] * mixed).astype(o_ref.dtype)


def gmlp(proj, norm_g, w_s, b_s, *, width, tm=256):
    n = proj.shape[0]
    groups = width // GM_GROUP_CH
    return pl.pallas_call(
        functools.partial(_gmlp_kernel, width=width),
        grid=(n // tm,),
        in_specs=[pl.BlockSpec((tm, 2 * width), lambda i: (i, 0)),
                  pl.BlockSpec((1, width), lambda i: (0, 0)),
                  pl.BlockSpec((groups, GM_CHUNK, GM_CHUNK), lambda i: (0, 0, 0)),
                  pl.BlockSpec((GM_CHUNK, groups), lambda i: (0, 0))],
        out_specs=pl.BlockSpec((tm, width), lambda i: (i, 0)),
        out_shape=jax.ShapeDtypeStruct((n, width), BF16),
        compiler_params=_params("parallel"),
        name="gmlp",
    )(proj, norm_g.reshape(1, width).astype(F32), w_s.astype(F32), jnp.transpose(b_s).astype(F32))


def _neg_softplus(z):
    return jnp.minimum(-z, 0.0) - jnp.log1p(jnp.exp(-jnp.abs(z)))


def _sb_attn_kernel(q_ref, k_ref, v_ref, t2_ref, o_ref, *, tq, tk, scale):
    qi = pl.program_id(2)
    q = q_ref[...]
    t2 = t2_ref[...]
    nd = tq // tk

    def block(j, carry, masked):
        run, acc = carry
        start = pl.multiple_of(j * tk, tk)
        k = k_ref[pl.ds(start, tk), :]
        v = v_ref[pl.ds(start, tk), :]
        z = lax.dot_general(q, k, (((1,), (1,)), ((), ())), preferred_element_type=F32) * scale
        lm = _neg_softplus(z)
        if masked:
            t_idx = qi * tq + lax.broadcasted_iota(jnp.int32, (tq, tk), 0)
            s_idx = j * tk + lax.broadcasted_iota(jnp.int32, (tq, tk), 1)
            mask = s_idx < t_idx
            lm = jnp.where(mask, lm, 0.0)
        hi = lm.astype(BF16)
        lo = (lm - hi.astype(F32)).astype(BF16)
        cs = jnp.dot(jnp.concatenate([hi, lo], axis=1), t2, preferred_element_type=F32)
        w = jnp.exp(z + cs[:, :tk] + run)
        if masked:
            w = jnp.where(mask, w, 0.0)
        acc = acc + jnp.dot(w.astype(BF16), v, preferred_element_type=F32)
        return run + cs[:, tk:], acc

    carry = (jnp.zeros((tq, tk), F32), jnp.zeros((tq, q.shape[1]), F32))
    last = (qi + 1) * nd - 1
    for d in range(nd):
        carry = block(last - d, carry, True)
    carry = lax.fori_loop(0, qi * nd, lambda i, c: block(qi * nd - 1 - i, c, False), carry)
    o_ref[...] = carry[1].astype(o_ref.dtype)


def sb_attention(proj, q_col, k_col, v_col, *, heads, tq=256, tk=128):
    bsz, seq, _ = proj.shape
    dh = SB_HEAD_DIM
    assert tk == dh, "the running total is carried lane-replicated at head-dim width"
    tq = min(tq, seq)
    j_idx = jnp.arange(tk)
    tri = (j_idx[:, None] >= j_idx[None, :]).astype(BF16)
    half = jnp.concatenate([tri, jnp.ones((tk, tk), BF16)], axis=1)
    t2 = jnp.concatenate([half, half], axis=0)
    kern = functools.partial(_sb_attn_kernel, tq=tq, tk=tk, scale=dh ** -0.5)
    return pl.pallas_call(
        kern,
        grid=(bsz, heads, seq // tq),
        in_specs=[pl.BlockSpec((None, tq, dh), lambda b, h, i: (b, i, q_col + h)),
                  pl.BlockSpec((None, seq, dh), lambda b, h, i: (b, 0, k_col + h)),
                  pl.BlockSpec((None, seq, dh), lambda b, h, i: (b, 0, v_col + h)),
                  pl.BlockSpec((2 * tk, 2 * tk), lambda b, h, i: (0, 0))],
        out_specs=pl.BlockSpec((None, tq, dh), lambda b, h, i: (b, i, h)),
        out_shape=jax.ShapeDtypeStruct((bsz, seq, heads * dh), BF16),
        compiler_params=_params("parallel", "parallel", "arbitrary"),
        name="sb_attention",
    )(proj, proj, proj, t2)


def _s5_kernel(x_ref, bmat_ref, cmat_ref, a_ref, at_ref, d_ref, wglu_ref, bglu_ref, o_ref,
               bu_ref, g_ref, fin_ref, sin_ref, carry_ref, *, tsub):
    nb = x_ref.shape[0]
    ns = S5_BUNDLE_STATE
    tc = tsub * S5_SEGMENTS

    @pl.when(pl.program_id(1) == 0)
    def _():
        carry_ref[...] = jnp.zeros_like(carry_ref)

    def cmul_add(ar, ai, sr, si, br, bi):
        return ar * sr - ai * si + br, ar * si + ai * sr + bi

    def bundle(j, _):
        x = x_ref[j]
        bu_ref[...] = jnp.dot(x, bmat_ref[j], preferred_element_type=F32)
        a = a_ref[j]
        ar = jnp.broadcast_to(a[:, :ns], (S5_SEGMENTS, ns))
        ai = jnp.broadcast_to(a[:, ns:], (S5_SEGMENTS, ns))

        def load(i):
            r0 = pl.multiple_of(i * S5_SEGMENTS, S5_SEGMENTS)
            return bu_ref[pl.ds(r0, S5_SEGMENTS), :ns], bu_ref[pl.ds(r0, S5_SEGMENTS), ns:]

        def step1(i, s):
            br, bi = load(i)
            return cmul_add(ar, ai, s[0], s[1], br, bi)

        zero = jnp.zeros((S5_SEGMENTS, ns), F32)
        fr, fi = lax.fori_loop(0, tsub, step1, (zero, zero), unroll=4)
        fin_ref[:, :ns] = fr
        fin_ref[:, ns:] = fi
        at = at_ref[j]
        atr, ati = at[:, :ns], at[:, ns:]
        pr = carry_ref[j, :, :ns]
        pi = carry_ref[j, :, ns:]
        for k in range(S5_SEGMENTS):
            sin_ref[k:k + 1, :ns] = pr
            sin_ref[k:k + 1, ns:] = pi
            pr, pi = cmul_add(atr, ati, pr, pi, fin_ref[k:k + 1, :ns], fin_ref[k:k + 1, ns:])
        carry_ref[j, :, :ns] = pr
        carry_ref[j, :, ns:] = pi

        def step2(i, s):
            br, bi = load(i)
            sr, si = cmul_add(ar, ai, s[0], s[1], br, bi)
            r0 = pl.multiple_of(i * S5_SEGMENTS, S5_SEGMENTS)
            bu_ref[pl.ds(r0, S5_SEGMENTS), :ns] = sr
            bu_ref[pl.ds(r0, S5_SEGMENTS), ns:] = si
            return sr, si

        lax.fori_loop(0, tsub, step2, (sin_ref[:, :ns], sin_ref[:, ns:]), unroll=4)
        y = jnp.dot(bu_ref[...].astype(BF16), cmat_ref[j], preferred_element_type=F32)
        y = y + d_ref[j] * x.astype(F32)
        g_ref[j] = jax.nn.gelu(y)
        return 0

    lax.fori_loop(0, nb, bundle, 0)

    gate = jnp.zeros((tc, nb * S5_BUNDLE_CH), F32)
    for j in range(nb):
        gate = gate + jnp.dot(g_ref[j].astype(BF16), wglu_ref[j], preferred_element_type=F32)
    gate = jax.nn.sigmoid(gate + bglu_ref[...])
    for j in range(nb):
        cols = slice(j * S5_BUNDLE_CH, (j + 1) * S5_BUNDLE_CH)
        o_ref[:, cols] = (g_ref[j] * gate[:, cols]).astype(o_ref.dtype)


def _s5_tables(lam_re, lam_im, log_dt, b_re, b_im, c_re, c_im, tsub):
    groups = lam_re.shape[0]
    nb = groups // S5_BUNDLE_GROUPS
    dt = jnp.exp(log_dt.astype(F32))[:, None]
    lam_re = lam_re.astype(F32)
    lam_im = lam_im.astype(F32)
    mag = jnp.exp(lam_re * dt)
    ab_re = mag * jnp.cos(lam_im * dt)
    ab_im = mag * jnp.sin(lam_im * dt)
    den = lam_re * lam_re + lam_im * lam_im
    n_re = ab_re - 1.0
    n_im = ab_im
    k_re = (n_re * lam_re + n_im * lam_im) / den
    k_im = (n_im * lam_re - n_re * lam_im) / den
    b_re = b_re.astype(F32)
    b_im = b_im.astype(F32)
    bb_re = k_re[..., None] * b_re - k_im[..., None] * b_im
    bb_im = k_re[..., None] * b_im + k_im[..., None] * b_re
    magt = jnp.exp(lam_re * dt * tsub)
    at_re = magt * jnp.cos(lam_im * dt * tsub)
    at_im = magt * jnp.sin(lam_im * dt * tsub)
    eye = jnp.eye(S5_BUNDLE_GROUPS, dtype=F32)

    def pack_b(bb):
        bb = bb.reshape(nb, S5_BUNDLE_GROUPS, S5_STATE, S5_GROUP_CH)
        return jnp.einsum("jgph,gk->jghkp", bb, eye).reshape(nb, S5_BUNDLE_CH, S5_BUNDLE_STATE)

    def pack_c(c):
        c = c.astype(F32).reshape(nb, S5_BUNDLE_GROUPS, S5_GROUP_CH, S5_STATE)
        return jnp.einsum("jghp,gk->jkpgh", c, eye).reshape(nb, S5_BUNDLE_STATE, S5_BUNDLE_CH)

    def pack_a(re, im):
        return jnp.concatenate([re.reshape(nb, 1, S5_BUNDLE_STATE), im.reshape(nb, 1, S5_BUNDLE_STATE)], axis=-1)

    bmat = jnp.concatenate([pack_b(bb_re), pack_b(bb_im)], axis=-1).astype(BF16)
    cmat = jnp.concatenate([pack_c(c_re), -pack_c(c_im)], axis=1).astype(BF16)
    return bmat, cmat, pack_a(ab_re, ab_im), pack_a(at_re, at_im)


def s5_mixer(xin, lam_re, lam_im, log_dt, b_re, b_im, c_re, c_im, d, w_glu, b_glu, *, tc=512):
    bsz, seq, width = xin.shape
    tc = min(tc, seq)
    tsub = tc // S5_SEGMENTS
    nb = width // S5_BUNDLE_CH
    nchunk = seq // tc
    bmat, cmat, a, at = _s5_tables(lam_re, lam_im, log_dt, b_re, b_im, c_re, c_im, tsub)
    xp = xin.reshape(bsz, nchunk, S5_SEGMENTS, tsub, nb, S5_BUNDLE_CH)
    xp = jnp.transpose(xp, (0, 4, 1, 3, 2, 5)).reshape(bsz, nb, seq, S5_BUNDLE_CH)
    ns2 = 2 * S5_BUNDLE_STATE
    out = pl.pallas_call(
        functools.partial(_s5_kernel, tsub=tsub),
        grid=(bsz, nchunk),
        in_specs=[pl.BlockSpec((None, nb, tc, S5_BUNDLE_CH), lambda b, c: (b, 0, c, 0)),
                  pl.BlockSpec((nb, S5_BUNDLE_CH, ns2), lambda b, c: (0, 0, 0)),
                  pl.BlockSpec((nb, ns2, S5_BUNDLE_CH), lambda b, c: (0, 0, 0)),
                  pl.BlockSpec((nb, 1, ns2), lambda b, c: (0, 0, 0)),
                  pl.BlockSpec((nb, 1, ns2), lambda b, c: (0, 0, 0)),
                  pl.BlockSpec((nb, 1, S5_BUNDLE_CH), lambda b, c: (0, 0, 0)),
                  pl.BlockSpec((nb, S5_BUNDLE_CH, width), lambda b, c: (0, 0, 0)),
                  pl.BlockSpec((1, width), lambda b, c: (0, 0))],
        out_specs=pl.BlockSpec((None, tc, width), lambda b, c: (b, c, 0)),
        out_shape=jax.ShapeDtypeStruct((bsz, seq, width), BF16),
        scratch_shapes=[pltpu.VMEM((tc, ns2), F32),
                        pltpu.VMEM((nb, tc, S5_BUNDLE_CH), F32),
                        pltpu.VMEM((S5_SEGMENTS, ns2), F32),
                        pltpu.VMEM((S5_SEGMENTS, ns2), F32),
                        pltpu.VMEM((nb, 1, ns2), F32)],
        compiler_params=_params("parallel", "arbitrary"),
        name="s5_mixer",
    )(xp, bmat, cmat, a, at, d.reshape(nb, 1, S5_BUNDLE_CH).astype(F32),
      w_glu.astype(BF16).reshape(nb, S5_BUNDLE_CH, width), b_glu.reshape(1, width).astype(F32))
    out = out.reshape(bsz, nchunk, tsub, S5_SEGMENTS, width)
    return jnp.transpose(out, (0, 1, 3, 2, 4)).reshape(bsz, seq, width)


def _merge_kernel(ya_ref, yb_ref, yc_ref, ga_ref, gb_ref, gc_ref, w_ref, o_ref):
    acc = ga_ref[...].astype(F32) * jnp.dot(ya_ref[...], w_ref[0], preferred_element_type=F32)
    acc += gb_ref[...].astype(F32) * jnp.dot(yb_ref[...], w_ref[1], preferred_element_type=F32)
    acc += gc_ref[...].astype(F32) * jnp.dot(yc_ref[...], w_ref[2], preferred_element_type=F32)
    o_ref[...] = acc.astype(o_ref.dtype)


def merge_branches(ya, yb, yc, gates, w_branch, *, tm=512, tn=512):
    n, width = ya.shape
    d = w_branch.shape[2]
    nj = d // tn
    y_spec = pl.BlockSpec((tm, width), lambda i, j: (i, 0))

    def gate_spec(branch):
        return pl.BlockSpec((tm, tn), lambda i, j: (i, branch * nj + j))

    return pl.pallas_call(
        _merge_kernel,
        grid=(n // tm, nj),
        in_specs=[y_spec, y_spec, y_spec, gate_spec(0), gate_spec(1), gate_spec(2),
                  pl.BlockSpec((3, width, tn), lambda i, j: (0, 0, j))],
        out_specs=pl.BlockSpec((tm, tn), lambda i, j: (i, j)),
        out_shape=jax.ShapeDtypeStruct((n, d), BF16),
        compiler_params=_params("parallel", "parallel"),
        name="merge_branches",
    )(ya, yb, yc, gates, gates, gates, w_branch)


def kernel(x, norm1_g, w_in, b_gate, gm_norm_g, gm_w_s, gm_b_s, s5_lambda_re, s5_lambda_im, s5_log_dt,
           s5_b_re, s5_b_im, s5_c_re, s5_c_im, s5_d, s5_w_glu, s5_b_glu, w_branch, w_out, norm2_g,
           w_mlp_in, w_mlp_out, final_g):
    bsz, seq, d_model = x.shape
    depth = w_in.shape[0]
    width = gm_norm_g.shape[1]
    heads = width // SB_HEAD_DIM
    o_a = 2 * width
    o_b = o_a + width
    o_c = o_b + 3 * width
    n = bsz * seq
    xf = x.reshape(n, d_model)
    for l in range(depth):
        h = rmsnorm(xf, norm1_g[l], BF16)
        w_l = w_in[l].astype(BF16)
        proj = matmul(h, w_l[:, :o_c])
        gates = matmul(h, w_l[:, o_c:], bias=b_gate[l].reshape(-1), act="sigmoid")
        y_a = gmlp(proj, gm_norm_g[l], gm_w_s[l], gm_b_s[l], width=width)
        proj3 = proj.reshape(bsz, seq, o_c)
        y_b = s5_mixer(proj3[:, :, o_a:o_b], s5_lambda_re[l], s5_lambda_im[l], s5_log_dt[l],
                       s5_b_re[l], s5_b_im[l], s5_c_re[l], s5_c_im[l], s5_d[l], s5_w_glu[l], s5_b_glu[l])
        qc = o_b // SB_HEAD_DIM
        y_c = sb_attention(proj3, qc, qc + heads, qc + 2 * heads, heads=heads)
        merged = merge_branches(y_a, y_b.reshape(n, width), y_c.reshape(n, width), gates,
                                w_branch[l].astype(BF16))
        xf = matmul(merged, w_out[l].astype(BF16), residual=xf, out_dtype=F32)
        h2 = rmsnorm(xf, norm2_g[l], BF16)
        act = matmul(h2, w_mlp_in[l].astype(BF16), act="relu2")
        xf = matmul(act, w_mlp_out[l].astype(BF16), residual=xf, out_dtype=F32)
    return rmsnorm(xf, final_g, x.dtype).reshape(bsz, seq, d_model)
```

```python
import functools
import math

import jax
import jax.numpy as jnp
from jax import lax
from jax.experimental import pallas as pl
from jax.experimental.pallas import tpu as pltpu

F32 = jnp.float32
BF16 = jnp.bfloat16

EPS = 1e-6
LANES = 128
SUBLANES = 8
VMEM_LIMIT_BYTES = 56 * 1024 * 1024

GM_CHUNK = 128
GM_GROUP_CH = 128
S5_GROUP_CH = 16
S5_STATE = 64
S5_BUNDLE_GROUPS = 8
S5_BUNDLE_CH = S5_BUNDLE_GROUPS * S5_GROUP_CH
S5_BUNDLE_STATE = S5_BUNDLE_GROUPS * S5_STATE
S5_SEGMENTS = SUBLANES
SB_HEAD_DIM = 128


def _params(*semantics):
    return pltpu.CompilerParams(dimension_semantics=semantics, vmem_limit_bytes=VMEM_LIMIT_BYTES)


def _rmsnorm_kernel(x_ref, g_ref, o_ref):
    x = x_ref[...]
    y = x * lax.rsqrt(jnp.mean(x * x, axis=-1, keepdims=True) + EPS)
    o_ref[...] = (y * g_ref[...]).astype(o_ref.dtype)


def rmsnorm(x, g, out_dtype, tm=512):
    n, d = x.shape
    tm = min(tm, n)
    return pl.pallas_call(
        _rmsnorm_kernel,
        grid=(n // tm,),
        in_specs=[pl.BlockSpec((tm, d), lambda i: (i, 0)),
                  pl.BlockSpec((1, d), lambda i: (0, 0))],
        out_specs=pl.BlockSpec((tm, d), lambda i: (i, 0)),
        out_shape=jax.ShapeDtypeStruct((n, d), out_dtype),
        compiler_params=_params("parallel"),
        name="rmsnorm",
    )(x, g.reshape(1, d).astype(F32))


def _matmul_kernel(*refs, nk, act, has_bias, has_res):
    x_ref, w_ref = refs[0], refs[1]
    pos = 2
    b_ref = r_ref = None
    if has_bias:
        b_ref = refs[pos]
        pos += 1
    if has_res:
        r_ref = refs[pos]
        pos += 1
    o_ref = refs[pos]
    acc_ref = refs[pos + 1] if nk > 1 else None

    def finish(acc):
        if has_bias:
            acc = acc + b_ref[...]
        if act == "sigmoid":
            acc = jax.nn.sigmoid(acc)
        elif act == "relu2":
            acc = jnp.square(jnp.maximum(acc, 0.0))
        if has_res:
            acc = acc + r_ref[...]
        o_ref[...] = acc.astype(o_ref.dtype)

    part = jnp.dot(x_ref[...], w_ref[...], preferred_element_type=F32)
    if nk == 1:
        finish(part)
        return
    k = pl.program_id(2)

    @pl.when(k == 0)
    def _():
        acc_ref[...] = part

    @pl.when(k > 0)
    def _():
        acc_ref[...] += part

    @pl.when(k == nk - 1)
    def _():
        finish(acc_ref[...])


def matmul(x, w, *, bias=None, act=None, residual=None, out_dtype=BF16, tm=1024, tn=1024, tk=2048):
    m, kdim = x.shape
    n = w.shape[1]
    tm, tn, tk = min(tm, m), min(tn, n), min(tk, kdim)
    nk = kdim // tk
    in_specs = [pl.BlockSpec((tm, tk), lambda i, j, k: (i, k)),
                pl.BlockSpec((tk, tn), lambda i, j, k: (k, j))]
    args = [x, w]
    if bias is not None:
        in_specs.append(pl.BlockSpec((1, tn), lambda i, j, k: (0, j)))
        args.append(bias.reshape(1, n).astype(F32))
    if residual is not None:
        in_specs.append(pl.BlockSpec((tm, tn), lambda i, j, k: (i, j)))
        args.append(residual)
    kern = functools.partial(_matmul_kernel, nk=nk, act=act, has_bias=bias is not None,
                             has_res=residual is not None)
    return pl.pallas_call(
        kern,
        grid=(m // tm, n // tn, nk),
        in_specs=in_specs,
        out_specs=pl.BlockSpec((tm, tn), lambda i, j, k: (i, j)),
        out_shape=jax.ShapeDtypeStruct((m, n), out_dtype),
        scratch_shapes=[pltpu.VMEM((tm, tn), F32)] if nk > 1 else [],
        compiler_params=_params("parallel", "parallel", "arbitrary"),
        name="matmul",
    )(*args)


def _gmlp_kernel(uv_ref, g_ref, w_ref, bt_ref, o_ref, *, width):
    z = jax.nn.gelu(uv_ref[...].astype(F32))
    u = z[:, :width]
    v = z[:, width:]
    v = v * lax.rsqrt(jnp.mean(v * v, axis=-1, keepdims=True) + EPS) * g_ref[...]
    v = v.astype(BF16)
    tm = u.shape[0]
    t_idx = lax.broadcasted_iota(jnp.int32, (GM_CHUNK, GM_CHUNK), 0)
    s_idx = lax.broadcasted_iota(jnp.int32, (GM_CHUNK, GM_CHUNK), 1)
    causal = s_idx <= t_idx
    for g in range(width // GM_GROUP_CH):
        w = jnp.where(causal, w_ref[g], 0.0).astype(BF16)
        bias = bt_ref[:, g:g + 1]
        cols = slice(g * GM_GROUP_CH, (g + 1) * GM_GROUP_CH)
        for c in range(tm // GM_CHUNK):
            rows = slice(c * GM_CHUNK, (c + 1) * GM_CHUNK)
            mixed = jnp.dot(w, v[rows, cols], preferred_element_type=F32) + bias
            o_ref[rows, cols] = (u[rows, cols] * mixed).astype(o_ref.dtype)


def gmlp(proj, norm_g, w_s, b_s, *, width, tm=256):
    n = proj.shape[0]
    groups = width // GM_GROUP_CH
    return pl.pallas_call(
        functools.partial(_gmlp_kernel, width=width),
        grid=(n // tm,),
        in_specs=[pl.BlockSpec((tm, 2 * width), lambda i: (i, 0)),
                  pl.BlockSpec((1, width), lambda i: (0, 0)),
                  pl.BlockSpec((groups, GM_CHUNK, GM_CHUNK), lambda i: (0, 0, 0)),
                  pl.BlockSpec((GM_CHUNK, groups), lambda i: (0, 0))],
        out_specs=pl.BlockSpec((tm, width), lambda i: (i, 0)),
        out_shape=jax.ShapeDtypeStruct((n, width), BF16),
        compiler_params=_params("parallel"),
        name="gmlp",
    )(proj, norm_g.reshape(1, width).astype(F32), w_s.astype(F32), jnp.transpose(b_s).astype(F32))


SB_LOGIT_SCALE = SB_HEAD_DIM ** -0.5 * math.log2(math.e)


def _sb_attn_kernel(q_ref, k_ref, v_ref, t2_ref, o_ref, *, tq, blk, hps):
    qi = pl.program_id(2)
    dh = SB_HEAD_DIM
    t2 = t2_ref[...]
    nblk = tq // blk

    def head_tile(q, k, v, run, acc, masked):
        z = lax.dot_general(q, k, (((1,), (1,)), ((), ())), preferred_element_type=F32)
        neg_abs = pltpu.bitcast(pltpu.bitcast(z, jnp.uint32) | jnp.uint32(0x80000000), F32)
        p = jnp.maximum(z, 0.0) + jnp.log2(1.0 + jnp.exp2(neg_abs))
        if masked:
            t_idx = lax.broadcasted_iota(jnp.int32, (tq, tq), 0)
            s_idx = lax.broadcasted_iota(jnp.int32, (tq, tq), 1)
            mask = s_idx < t_idx
            p = jnp.where(mask, p, 0.0)
        hi = p.astype(BF16)
        lo = (p - hi.astype(F32)).astype(BF16)
        ws = [None] * nblk
        for b in reversed(range(nblk)):
            cols = slice(b * blk, (b + 1) * blk)
            cs = jnp.dot(jnp.concatenate([hi[:, cols], lo[:, cols]], axis=1), t2, preferred_element_type=F32)
            w = jnp.exp2(z[:, cols] - cs[:, :blk] - run)
            if masked:
                w = jnp.where(mask[:, cols], w, 0.0)
            ws[b] = w.astype(BF16)
            run = run + cs[:, blk:]
        acc = acc + jnp.dot(jnp.concatenate(ws, axis=1), v, preferred_element_type=F32)
        return run, acc

    def tile(start, carry, masked):
        out = []
        for h in range(hps):
            cols = slice(h * dh, (h + 1) * dh)
            run, acc = carry[h]
            out.append(head_tile(q_ref[:, cols], k_ref[pl.ds(start, tq), cols], v_ref[pl.ds(start, tq), cols],
                                 run, acc, masked))
        return tuple(out)

    carry = tuple((jnp.zeros((tq, blk), F32), jnp.zeros((tq, dh), F32)) for _ in range(hps))
    carry = tile(pl.multiple_of(qi * tq, tq), carry, True)
    carry = lax.fori_loop(0, qi, lambda i, c: tile(pl.multiple_of((qi - 1 - i) * tq, tq), c, False), carry)
    for h in range(hps):
        o_ref[:, h * dh:(h + 1) * dh] = carry[h][1].astype(o_ref.dtype)


def sb_attention(proj, q_col, k_col, v_col, *, heads, tq=512, blk=128, hps=2):
    bsz, seq, _ = proj.shape
    hw = hps * SB_HEAD_DIM
    tq = min(tq, seq)
    j_idx = jnp.arange(blk)
    tri = (j_idx[:, None] >= j_idx[None, :]).astype(BF16)
    half = jnp.concatenate([tri, jnp.ones((blk, blk), BF16)], axis=1)
    t2 = jnp.concatenate([half, half], axis=0)
    kern = functools.partial(_sb_attn_kernel, tq=tq, blk=blk, hps=hps)
    return pl.pallas_call(
        kern,
        grid=(bsz, heads // hps, seq // tq),
        in_specs=[pl.BlockSpec((None, tq, hw), lambda b, h, i: (b, i, q_col // hw + h)),
                  pl.BlockSpec((None, seq, hw), lambda b, h, i: (b, 0, k_col // hw + h)),
                  pl.BlockSpec((None, seq, hw), lambda b, h, i: (b, 0, v_col // hw + h)),
                  pl.BlockSpec((2 * blk, 2 * blk), lambda b, h, i: (0, 0))],
        out_specs=pl.BlockSpec((None, tq, hw), lambda b, h, i: (b, i, h)),
        out_shape=jax.ShapeDtypeStruct((bsz, seq, heads * SB_HEAD_DIM), BF16),
        compiler_params=_params("parallel", "parallel", "arbitrary"),
        name="sb_attention",
    )(proj, proj, proj, t2)


def _s5_kernel(x_ref, bmat_ref, cmat_ref, a_ref, at_ref, d_ref, wglu_ref, bglu_ref, o_ref,
               bu_ref, g_ref, fin_ref, sin_ref, carry_ref, *, tsub):
    nb = x_ref.shape[0]
    ns = S5_BUNDLE_STATE
    tc = tsub * S5_SEGMENTS

    @pl.when(pl.program_id(1) == 0)
    def _():
        carry_ref[...] = jnp.zeros_like(carry_ref)

    def cmul_add(ar, ai, sr, si, br, bi):
        return ar * sr - ai * si + br, ar * si + ai * sr + bi

    def bundle(j, _):
        x = x_ref[j]
        bu_ref[...] = jnp.dot(x, bmat_ref[j], preferred_element_type=F32)
        a = a_ref[j]
        ar = jnp.broadcast_to(a[:, :ns], (S5_SEGMENTS, ns))
        ai = jnp.broadcast_to(a[:, ns:], (S5_SEGMENTS, ns))

        def load(i):
            r0 = pl.multiple_of(i * S5_SEGMENTS, S5_SEGMENTS)
            return bu_ref[pl.ds(r0, S5_SEGMENTS), :ns], bu_ref[pl.ds(r0, S5_SEGMENTS), ns:]

        def step1(i, s):
            br, bi = load(i)
            return cmul_add(ar, ai, s[0], s[1], br, bi)

        zero = jnp.zeros((S5_SEGMENTS, ns), F32)
        fr, fi = lax.fori_loop(0, tsub, step1, (zero, zero), unroll=4)
        fin_ref[:, :ns] = fr
        fin_ref[:, ns:] = fi
        at = at_ref[j]
        atr, ati = at[:, :ns], at[:, ns:]
        pr = carry_ref[j, :, :ns]
        pi = carry_ref[j, :, ns:]
        for k in range(S5_SEGMENTS):
            sin_ref[k:k + 1, :ns] = pr
            sin_ref[k:k + 1, ns:] = pi
            pr, pi = cmul_add(atr, ati, pr, pi, fin_ref[k:k + 1, :ns], fin_ref[k:k + 1, ns:])
        carry_ref[j, :, :ns] = pr
        carry_ref[j, :, ns:] = pi

        def step2(i, s):
            br, bi = load(i)
            sr, si = cmul_add(ar, ai, s[0], s[1], br, bi)
            r0 = pl.multiple_of(i * S5_SEGMENTS, S5_SEGMENTS)
            bu_ref[pl.ds(r0, S5_SEGMENTS), :ns] = sr
            bu_ref[pl.ds(r0, S5_SEGMENTS), ns:] = si
            return sr, si

        lax.fori_loop(0, tsub, step2, (sin_ref[:, :ns], sin_ref[:, ns:]), unroll=4)
        y = jnp.dot(bu_ref[...].astype(BF16), cmat_ref[j], preferred_element_type=F32)
        y = y + d_ref[j] * x.astype(F32)
        g_ref[j] = jax.nn.gelu(y)
        return 0

    lax.fori_loop(0, nb, bundle, 0)

    gate = jnp.zeros((tc, nb * S5_BUNDLE_CH), F32)
    for j in range(nb):
        gate = gate + jnp.dot(g_ref[j].astype(BF16), wglu_ref[j], preferred_element_type=F32)
    gate = jax.nn.sigmoid(gate + bglu_ref[...])
    for j in range(nb):
        cols = slice(j * S5_BUNDLE_CH, (j + 1) * S5_BUNDLE_CH)
        o_ref[:, cols] = (g_ref[j] * gate[:, cols]).astype(o_ref.dtype)


def _s5_tables(lam_re, lam_im, log_dt, b_re, b_im, c_re, c_im, tsub):
    groups = lam_re.shape[0]
    nb = groups // S5_BUNDLE_GROUPS
    dt = jnp.exp(log_dt.astype(F32))[:, None]
    lam_re = lam_re.astype(F32)
    lam_im = lam_im.astype(F32)
    mag = jnp.exp(lam_re * dt)
    ab_re = mag * jnp.cos(lam_im * dt)
    ab_im = mag * jnp.sin(lam_im * dt)
    den = lam_re * lam_re + lam_im * lam_im
    n_re = ab_re - 1.0
    n_im = ab_im
    k_re = (n_re * lam_re + n_im * lam_im) / den
    k_im = (n_im * lam_re - n_re * lam_im) / den
    b_re = b_re.astype(F32)
    b_im = b_im.astype(F32)
    bb_re = k_re[..., None] * b_re - k_im[..., None] * b_im
    bb_im = k_re[..., None] * b_im + k_im[..., None] * b_re
    magt = jnp.exp(lam_re * dt * tsub)
    at_re = magt * jnp.cos(lam_im * dt * tsub)
    at_im = magt * jnp.sin(lam_im * dt * tsub)
    eye = jnp.eye(S5_BUNDLE_GROUPS, dtype=F32)

    def pack_b(bb):
        bb = bb.reshape(nb, S5_BUNDLE_GROUPS, S5_STATE, S5_GROUP_CH)
        return jnp.einsum("jgph,gk->jghkp", bb, eye).reshape(nb, S5_BUNDLE_CH, S5_BUNDLE_STATE)

    def pack_c(c):
        c = c.astype(F32).reshape(nb, S5_BUNDLE_GROUPS, S5_GROUP_CH, S5_STATE)
        return jnp.einsum("jghp,gk->jkpgh", c, eye).reshape(nb, S5_BUNDLE_STATE, S5_BUNDLE_CH)

    def pack_a(re, im):
        return jnp.concatenate([re.reshape(nb, 1, S5_BUNDLE_STATE), im.reshape(nb, 1, S5_BUNDLE_STATE)], axis=-1)

    bmat = jnp.concatenate([pack_b(bb_re), pack_b(bb_im)], axis=-1).astype(BF16)
    cmat = jnp.concatenate([pack_c(c_re), -pack_c(c_im)], axis=1).astype(BF16)
    return bmat, cmat, pack_a(ab_re, ab_im), pack_a(at_re, at_im)


def s5_mixer(xin, lam_re, lam_im, log_dt, b_re, b_im, c_re, c_im, d, w_glu, b_glu, *, tc=512):
    bsz, seq, width = xin.shape
    tc = min(tc, seq)
    tsub = tc // S5_SEGMENTS
    nb = width // S5_BUNDLE_CH
    nchunk = seq // tc
    bmat, cmat, a, at = _s5_tables(lam_re, lam_im, log_dt, b_re, b_im, c_re, c_im, tsub)
    xp = xin.reshape(bsz, nchunk, S5_SEGMENTS, tsub, nb, S5_BUNDLE_CH)
    xp = jnp.transpose(xp, (0, 4, 1, 3, 2, 5)).reshape(bsz, nb, seq, S5_BUNDLE_CH)
    ns2 = 2 * S5_BUNDLE_STATE
    out = pl.pallas_call(
        functools.partial(_s5_kernel, tsub=tsub),
        grid=(bsz, nchunk),
        in_specs=[pl.BlockSpec((None, nb, tc, S5_BUNDLE_CH), lambda b, c: (b, 0, c, 0)),
                  pl.BlockSpec((nb, S5_BUNDLE_CH, ns2), lambda b, c: (0, 0, 0)),
                  pl.BlockSpec((nb, ns2, S5_BUNDLE_CH), lambda b, c: (0, 0, 0)),
                  pl.BlockSpec((nb, 1, ns2), lambda b, c: (0, 0, 0)),
                  pl.BlockSpec((nb, 1, ns2), lambda b, c: (0, 0, 0)),
                  pl.BlockSpec((nb, 1, S5_BUNDLE_CH), lambda b, c: (0, 0, 0)),
                  pl.BlockSpec((nb, S5_BUNDLE_CH, width), lambda b, c: (0, 0, 0)),
                  pl.BlockSpec((1, width), lambda b, c: (0, 0))],
        out_specs=pl.BlockSpec((None, tc, width), lambda b, c: (b, c, 0)),
        out_shape=jax.ShapeDtypeStruct((bsz, seq, width), BF16),
        scratch_shapes=[pltpu.VMEM((tc, ns2), F32),
                        pltpu.VMEM((nb, tc, S5_BUNDLE_CH), F32),
                        pltpu.VMEM((S5_SEGMENTS, ns2), F32),
                        pltpu.VMEM((S5_SEGMENTS, ns2), F32),
                        pltpu.VMEM((nb, 1, ns2), F32)],
        compiler_params=_params("parallel", "arbitrary"),
        name="s5_mixer",
    )(xp, bmat, cmat, a, at, d.reshape(nb, 1, S5_BUNDLE_CH).astype(F32),
      w_glu.astype(BF16).reshape(nb, S5_BUNDLE_CH, width), b_glu.reshape(1, width).astype(F32))
    out = out.reshape(bsz, nchunk, tsub, S5_SEGMENTS, width)
    return jnp.transpose(out, (0, 1, 3, 2, 4)).reshape(bsz, seq, width)


def _merge_kernel(ya_ref, yb_ref, yc_ref, ga_ref, gb_ref, gc_ref, w_ref, o_ref):
    acc = ga_ref[...].astype(F32) * jnp.dot(ya_ref[...], w_ref[0], preferred_element_type=F32)
    acc += gb_ref[...].astype(F32) * jnp.dot(yb_ref[...], w_ref[1], preferred_element_type=F32)
    acc += gc_ref[...].astype(F32) * jnp.dot(yc_ref[...], w_ref[2], preferred_element_type=F32)
    o_ref[...] = acc.astype(o_ref.dtype)


def merge_branches(ya, yb, yc, gates, w_branch, *, tm=512, tn=512):
    n, width = ya.shape
    d = w_branch.shape[2]
    nj = d // tn
    y_spec = pl.BlockSpec((tm, width), lambda i, j: (i, 0))

    def gate_spec(branch):
        return pl.BlockSpec((tm, tn), lambda i, j: (i, branch * nj + j))

    return pl.pallas_call(
        _merge_kernel,
        grid=(n // tm, nj),
        in_specs=[y_spec, y_spec, y_spec, gate_spec(0), gate_spec(1), gate_spec(2),
                  pl.BlockSpec((3, width, tn), lambda i, j: (0, 0, j))],
        out_specs=pl.BlockSpec((tm, tn), lambda i, j: (i, j)),
        out_shape=jax.ShapeDtypeStruct((n, d), BF16),
        compiler_params=_params("parallel", "parallel"),
        name="merge_branches",
    )(ya, yb, yc, gates, gates, gates, w_branch)


def kernel(x, norm1_g, w_in, b_gate, gm_norm_g, gm_w_s, gm_b_s, s5_lambda_re, s5_lambda_im, s5_log_dt,
           s5_b_re, s5_b_im, s5_c_re, s5_c_im, s5_d, s5_w_glu, s5_b_glu, w_branch, w_out, norm2_g,
           w_mlp_in, w_mlp_out, final_g):
    bsz, seq, d_model = x.shape
    depth = w_in.shape[0]
    width = gm_norm_g.shape[1]
    heads = width // SB_HEAD_DIM
    o_a = 2 * width
    o_b = o_a + width
    o_c = o_b + 3 * width
    n = bsz * seq
    xf = x.reshape(n, d_model)
    col = jnp.arange(w_in.shape[2])
    col_scale = jnp.where((col >= o_b) & (col < o_b + width), SB_LOGIT_SCALE, 1.0).astype(F32)
    for l in range(depth):
        h = rmsnorm(xf, norm1_g[l], BF16)
        w_l = (w_in[l] * col_scale).astype(BF16)
        proj = matmul(h, w_l[:, :o_c])
        gates = matmul(h, w_l[:, o_c:], bias=b_gate[l].reshape(-1), act="sigmoid")
        y_a = gmlp(proj, gm_norm_g[l], gm_w_s[l], gm_b_s[l], width=width)
        proj3 = proj.reshape(bsz, seq, o_c)
        y_b = s5_mixer(proj3[:, :, o_a:o_b], s5_lambda_re[l], s5_lambda_im[l], s5_log_dt[l],
                       s5_b_re[l], s5_b_im[l], s5_c_re[l], s5_c_im[l], s5_d[l], s5_w_glu[l], s5_b_glu[l])
        y_c = sb_attention(proj3, o_b, o_b + width, o_b + 2 * width, heads=heads)
        merged = merge_branches(y_a, y_b.reshape(n, width), y_c.reshape(n, width), gates,
                                w_branch[l].astype(BF16))
        xf = matmul(merged, w_out[l].astype(BF16), residual=xf, out_dtype=F32)
        h2 = rmsnorm(xf, norm2_g[l], BF16)
        act = matmul(h2, w_mlp_in[l].astype(BF16), act="relu2")
        xf = matmul(act, w_mlp_out[l].astype(BF16), residual=xf, out_dtype=F32)
    return rmsnorm(xf, final_g, x.dtype).reshape(bsz, seq, d_model)
```

```python
import functools
import math

import jax
import jax.numpy as jnp
from jax import lax
from jax.experimental import pallas as pl
from jax.experimental.pallas import tpu as pltpu

F32 = jnp.float32
BF16 = jnp.bfloat16

EPS = 1e-6
LANES = 128
SUBLANES = 8
VMEM_LIMIT_BYTES = 56 * 1024 * 1024

GM_CHUNK = 128
GM_GROUP_CH = 128
S5_GROUP_CH = 16
S5_STATE = 64
S5_BUNDLE_GROUPS = 8
S5_BUNDLE_CH = S5_BUNDLE_GROUPS * S5_GROUP_CH
S5_BUNDLE_STATE = S5_BUNDLE_GROUPS * S5_STATE
S5_SEGMENTS = SUBLANES
SB_HEAD_DIM = 128


def _params(*semantics):
    return pltpu.CompilerParams(dimension_semantics=semantics, vmem_limit_bytes=VMEM_LIMIT_BYTES)


def _rmsnorm_kernel(x_ref, g_ref, o_ref):
    x = x_ref[...]
    y = x * lax.rsqrt(jnp.mean(x * x, axis=-1, keepdims=True) + EPS)
    o_ref[...] = (y * g_ref[...]).astype(o_ref.dtype)


def rmsnorm(x, g, out_dtype, tm=512):
    n, d = x.shape
    tm = min(tm, n)
    return pl.pallas_call(
        _rmsnorm_kernel,
        grid=(n // tm,),
        in_specs=[pl.BlockSpec((tm, d), lambda i: (i, 0)),
                  pl.BlockSpec((1, d), lambda i: (0, 0))],
        out_specs=pl.BlockSpec((tm, d), lambda i: (i, 0)),
        out_shape=jax.ShapeDtypeStruct((n, d), out_dtype),
        compiler_params=_params("parallel"),
        name="rmsnorm",
    )(x, g.reshape(1, d).astype(F32))


def _matmul_kernel(*refs, nk, act, act_from, has_norm, has_affine, has_res):
    refs = list(refs)
    x_ref = refs.pop(0)
    g_ref = refs.pop(0) if has_norm else None
    w_ref = refs.pop(0)
    s_ref, b_ref = (refs.pop(0), refs.pop(0)) if has_affine else (None, None)
    r_ref = refs.pop(0) if has_res else None
    o_ref = refs.pop(0)
    h_ref = refs.pop(0) if has_norm else None
    acc_ref = refs.pop(0) if nk > 1 else None
    j = pl.program_id(1)

    if has_norm:
        @pl.when(j == 0)
        def _():
            x = x_ref[...]
            y = x * lax.rsqrt(jnp.mean(x * x, axis=-1, keepdims=True) + EPS)
            h_ref[...] = (y * g_ref[...]).astype(BF16)

        lhs = h_ref[...]
    else:
        lhs = x_ref[...]

    def finish(acc):
        if has_affine:
            acc = acc * s_ref[...] + b_ref[...]

        def store(val):
            if has_res:
                val = val + r_ref[...]
            o_ref[...] = val.astype(o_ref.dtype)

        def activated():
            if act == "sigmoid":
                return jax.nn.sigmoid(acc)
            return jnp.square(jnp.maximum(acc, 0.0))

        if act is None:
            store(acc)
        elif act_from == 0:
            store(activated())
        else:
            pl.when(j >= act_from)(lambda: store(activated()))
            pl.when(j < act_from)(lambda: store(acc))

    part = jnp.dot(lhs, w_ref[...].astype(BF16), preferred_element_type=F32)
    if nk == 1:
        finish(part)
        return
    k = pl.program_id(2)

    @pl.when(k == 0)
    def _():
        acc_ref[...] = part

    @pl.when(k > 0)
    def _():
        acc_ref[...] += part

    @pl.when(k == nk - 1)
    def _():
        finish(acc_ref[...])


def matmul(x, w, layer, *, norm_g=None, scale=None, bias=None, act=None, act_from_col=0, residual=None,
           out_dtype=BF16, tm=1024, tn=512, tk=2048):
    m, kdim = x.shape
    n = w.shape[2]
    tm, tn, tk = min(tm, m), min(tn, n), min(tk, kdim)
    nk = kdim // tk
    has_norm = norm_g is not None
    has_affine = scale is not None or bias is not None
    assert not has_norm or nk == 1, "the fused rmsnorm needs whole rows"
    assert act_from_col % tn == 0
    in_specs = [pl.BlockSpec((tm, tk), lambda i, j, k: (i, k))]
    args = [x]
    if has_norm:
        in_specs.append(pl.BlockSpec((1, kdim), lambda i, j, k: (0, 0)))
        args.append(norm_g.reshape(1, kdim).astype(F32))
    in_specs.append(pl.BlockSpec((None, tk, tn), lambda i, j, k: (layer, k, j)))
    args.append(w)
    if has_affine:
        row_spec = pl.BlockSpec((1, tn), lambda i, j, k: (0, j))
        in_specs += [row_spec, row_spec]
        args.append((jnp.ones((n,), F32) if scale is None else scale.astype(F32)).reshape(1, n))
        args.append((jnp.zeros((n,), F32) if bias is None else bias.astype(F32)).reshape(1, n))
    if residual is not None:
        in_specs.append(pl.BlockSpec((tm, tn), lambda i, j, k: (i, j)))
        args.append(residual)
    scratch = []
    if has_norm:
        scratch.append(pltpu.VMEM((tm, kdim), BF16))
    if nk > 1:
        scratch.append(pltpu.VMEM((tm, tn), F32))
    kern = functools.partial(_matmul_kernel, nk=nk, act=act, act_from=act_from_col // tn, has_norm=has_norm,
                             has_affine=has_affine, has_res=residual is not None)
    return pl.pallas_call(
        kern,
        grid=(m // tm, n // tn, nk),
        in_specs=in_specs,
        out_specs=pl.BlockSpec((tm, tn), lambda i, j, k: (i, j)),
        out_shape=jax.ShapeDtypeStruct((m, n), out_dtype),
        scratch_shapes=scratch,
        compiler_params=_params("parallel", "arbitrary", "arbitrary"),
        name="matmul",
    )(*args)


def _gmlp_kernel(uv_ref, g_ref, w_ref, bt_ref, o_ref, *, width):
    z = jax.nn.gelu(uv_ref[...].astype(F32))
    u = z[:, :width]
    v = z[:, width:]
    v = v * lax.rsqrt(jnp.mean(v * v, axis=-1, keepdims=True) + EPS) * g_ref[...]
    v = v.astype(BF16)
    tm = u.shape[0]
    t_idx = lax.broadcasted_iota(jnp.int32, (GM_CHUNK, GM_CHUNK), 0)
    s_idx = lax.broadcasted_iota(jnp.int32, (GM_CHUNK, GM_CHUNK), 1)
    causal = s_idx <= t_idx
    for g in range(width // GM_GROUP_CH):
        w = jnp.where(causal, w_ref[g], 0.0).astype(BF16)
        bias = bt_ref[:, g:g + 1]
        cols = slice(g * GM_GROUP_CH, (g + 1) * GM_GROUP_CH)
        for c in range(tm // GM_CHUNK):
            rows = slice(c * GM_CHUNK, (c + 1) * GM_CHUNK)
            mixed = jnp.dot(w, v[rows, cols], preferred_element_type=F32) + bias
            o_ref[rows, cols] = (u[rows, cols] * mixed).astype(o_ref.dtype)


def gmlp(proj, norm_g, w_s, b_s, *, width, tm=256):
    n = proj.shape[0]
    groups = width // GM_GROUP_CH
    return pl.pallas_call(
        functools.partial(_gmlp_kernel, width=width),
        grid=(n // tm,),
        in_specs=[pl.BlockSpec((tm, 2 * width), lambda i: (i, 0)),
                  pl.BlockSpec((1, width), lambda i: (0, 0)),
                  pl.BlockSpec((groups, GM_CHUNK, GM_CHUNK), lambda i: (0, 0, 0)),
                  pl.BlockSpec((GM_CHUNK, groups), lambda i: (0, 0))],
        out_specs=pl.BlockSpec((tm, width), lambda i: (i, 0)),
        out_shape=jax.ShapeDtypeStruct((n, width), BF16),
        compiler_params=_params("parallel"),
        name="gmlp",
    )(proj, norm_g.reshape(1, width).astype(F32), w_s.astype(F32), jnp.transpose(b_s).astype(F32))


SB_LOGIT_SCALE = SB_HEAD_DIM ** -0.5 * math.log2(math.e)


SB_MASKED_LOGIT = -1e30


def _sb_attn_kernel(q_ref, k_ref, v_ref, tri_ref, o_ref, z_scr, p_scr, tot_scr, *, tq, blk, hps):
    qi = pl.program_id(2)
    dh = SB_HEAD_DIM
    tri = tri_ref[...]
    nblk = tq // blk

    def logits(h, start, c, diagonal):
        hc = slice(h * dh, (h + 1) * dh)
        k = k_ref[pl.ds(pl.multiple_of(start + c * blk, blk), blk), hc]
        z = lax.dot_general(q_ref[:, hc], k, (((1,), (1,)), ((), ())), preferred_element_type=F32)
        if diagonal:
            t_idx = lax.broadcasted_iota(jnp.int32, (tq, blk), 0)
            s_idx = c * blk + lax.broadcasted_iota(jnp.int32, (tq, blk), 1)
            z = jnp.where(s_idx < t_idx, z, SB_MASKED_LOGIT)
        neg_abs = pltpu.bitcast(pltpu.bitcast(z, jnp.uint32) | jnp.uint32(0x80000000), F32)
        p = jnp.maximum(z, 0.0) + jnp.log2(1.0 + jnp.exp2(neg_abs))
        return z - p, p.astype(BF16), jnp.sum(p, axis=1, keepdims=True)

    def stash(h, c, vals):
        z_scr[h, c], p_scr[h, c], tot_scr[h, c] = vals

    def weights_step(h, start, c, run, acc):
        hc = slice(h * dh, (h + 1) * dh)
        cs = jnp.dot(p_scr[h, c], tri, preferred_element_type=F32)
        w = jnp.exp2(z_scr[h, c] - cs - run).astype(BF16)
        v = v_ref[pl.ds(pl.multiple_of(start + c * blk, blk), blk), hc]
        return run + tot_scr[h, c], acc + jnp.dot(w, v, preferred_element_type=F32)

    for h in range(hps):
        for c in reversed(range(nblk)):
            stash(h, c, logits(h, qi * tq, c, True))

    def trip(i, carry):
        cur = (qi - i) * tq
        out = []
        for h in range(hps):
            run, acc = carry[h]
            for c in reversed(range(nblk)):
                run, acc = weights_step(h, cur, c, run, acc)
                stash(h, c, logits(h, cur - tq, c, False))
            out.append((run, acc))
        return tuple(out)

    carry = tuple((jnp.zeros((tq, 1), F32), jnp.zeros((tq, dh), F32)) for _ in range(hps))
    carry = lax.fori_loop(0, qi, trip, carry)
    for h in range(hps):
        run, acc = carry[h]
        for c in reversed(range(nblk)):
            run, acc = weights_step(h, 0, c, run, acc)
        o_ref[:, h * dh:(h + 1) * dh] = acc.astype(o_ref.dtype)


def sb_attention(proj, q_col, k_col, v_col, *, heads, tq=512, blk=256, hps=2):
    bsz, seq, _ = proj.shape
    hw = hps * SB_HEAD_DIM
    tq = min(tq, seq)
    j_idx = jnp.arange(blk)
    tri = (j_idx[:, None] > j_idx[None, :]).astype(BF16)
    kern = functools.partial(_sb_attn_kernel, tq=tq, blk=blk, hps=hps)
    return pl.pallas_call(
        kern,
        grid=(bsz, heads // hps, seq // tq),
        in_specs=[pl.BlockSpec((None, tq, hw), lambda b, h, i: (b, i, q_col // hw + h)),
                  pl.BlockSpec((None, seq, hw), lambda b, h, i: (b, 0, k_col // hw + h)),
                  pl.BlockSpec((None, seq, hw), lambda b, h, i: (b, 0, v_col // hw + h)),
                  pl.BlockSpec((blk, blk), lambda b, h, i: (0, 0))],
        out_specs=pl.BlockSpec((None, tq, hw), lambda b, h, i: (b, i, h)),
        out_shape=jax.ShapeDtypeStruct((bsz, seq, heads * SB_HEAD_DIM), BF16),
        scratch_shapes=[pltpu.VMEM((hps, tq // blk, tq, blk), F32),
                        pltpu.VMEM((hps, tq // blk, tq, blk), BF16),
                        pltpu.VMEM((hps, tq // blk, tq, 1), F32)],
        compiler_params=_params("parallel", "parallel", "arbitrary"),
        name="sb_attention",
    )(proj, proj, proj, tri)


def _s5_kernel(x_ref, bmat_ref, cmat_ref, a_ref, at_ref, d_ref, wglu_ref, bglu_ref, o_ref,
               bu_ref, g_ref, fin_ref, sin_ref, carry_ref, *, tsub):
    nb = x_ref.shape[0]
    ns = S5_BUNDLE_STATE
    tc = tsub * S5_SEGMENTS

    @pl.when(pl.program_id(1) == 0)
    def _():
        carry_ref[...] = jnp.zeros_like(carry_ref)

    def cmul_add(ar, ai, sr, si, br, bi):
        return ar * sr - ai * si + br, ar * si + ai * sr + bi

    def bundle(j, _):
        x = x_ref[j]
        bu_ref[...] = jnp.dot(x, bmat_ref[j], preferred_element_type=F32)
        a = a_ref[j]
        ar = jnp.broadcast_to(a[:, :ns], (S5_SEGMENTS, ns))
        ai = jnp.broadcast_to(a[:, ns:], (S5_SEGMENTS, ns))

        def load(i):
            r0 = pl.multiple_of(i * S5_SEGMENTS, S5_SEGMENTS)
            return bu_ref[pl.ds(r0, S5_SEGMENTS), :ns], bu_ref[pl.ds(r0, S5_SEGMENTS), ns:]

        def step1(i, s):
            br, bi = load(i)
            return cmul_add(ar, ai, s[0], s[1], br, bi)

        zero = jnp.zeros((S5_SEGMENTS, ns), F32)
        fr, fi = lax.fori_loop(0, tsub, step1, (zero, zero), unroll=4)
        fin_ref[:, :ns] = fr
        fin_ref[:, ns:] = fi
        at = at_ref[j]
        atr, ati = at[:, :ns], at[:, ns:]
        pr = carry_ref[j, :, :ns]
        pi = carry_ref[j, :, ns:]
        for k in range(S5_SEGMENTS):
            sin_ref[k:k + 1, :ns] = pr
            sin_ref[k:k + 1, ns:] = pi
            pr, pi = cmul_add(atr, ati, pr, pi, fin_ref[k:k + 1, :ns], fin_ref[k:k + 1, ns:])
        carry_ref[j, :, :ns] = pr
        carry_ref[j, :, ns:] = pi

        def step2(i, s):
            br, bi = load(i)
            sr, si = cmul_add(ar, ai, s[0], s[1], br, bi)
            r0 = pl.multiple_of(i * S5_SEGMENTS, S5_SEGMENTS)
            bu_ref[pl.ds(r0, S5_SEGMENTS), :ns] = sr
            bu_ref[pl.ds(r0, S5_SEGMENTS), ns:] = si
            return sr, si

        lax.fori_loop(0, tsub, step2, (sin_ref[:, :ns], sin_ref[:, ns:]), unroll=4)
        y = jnp.dot(bu_ref[...].astype(BF16), cmat_ref[j], preferred_element_type=F32)
        y = y + d_ref[j] * x.astype(F32)
        g_ref[j] = jax.nn.gelu(y)
        return 0

    lax.fori_loop(0, nb, bundle, 0)

    gate = jnp.zeros((tc, nb * S5_BUNDLE_CH), F32)
    for j in range(nb):
        gate = gate + jnp.dot(g_ref[j].astype(BF16), wglu_ref[j], preferred_element_type=F32)
    gate = jax.nn.sigmoid(gate + bglu_ref[...])
    for j in range(nb):
        cols = slice(j * S5_BUNDLE_CH, (j + 1) * S5_BUNDLE_CH)
        o_ref[:, cols] = (g_ref[j] * gate[:, cols]).astype(o_ref.dtype)


def _s5_tables(lam_re, lam_im, log_dt, b_re, b_im, c_re, c_im, tsub):
    groups = lam_re.shape[0]
    nb = groups // S5_BUNDLE_GROUPS
    dt = jnp.exp(log_dt.astype(F32))[:, None]
    lam_re = lam_re.astype(F32)
    lam_im = lam_im.astype(F32)
    mag = jnp.exp(lam_re * dt)
    ab_re = mag * jnp.cos(lam_im * dt)
    ab_im = mag * jnp.sin(lam_im * dt)
    den = lam_re * lam_re + lam_im * lam_im
    n_re = ab_re - 1.0
    n_im = ab_im
    k_re = (n_re * lam_re + n_im * lam_im) / den
    k_im = (n_im * lam_re - n_re * lam_im) / den
    b_re = b_re.astype(F32)
    b_im = b_im.astype(F32)
    bb_re = k_re[..., None] * b_re - k_im[..., None] * b_im
    bb_im = k_re[..., None] * b_im + k_im[..., None] * b_re
    magt = jnp.exp(lam_re * dt * tsub)
    at_re = magt * jnp.cos(lam_im * dt * tsub)
    at_im = magt * jnp.sin(lam_im * dt * tsub)
    eye = jnp.eye(S5_BUNDLE_GROUPS, dtype=F32)

    def pack_b(bb):
        bb = bb.reshape(nb, S5_BUNDLE_GROUPS, S5_STATE, S5_GROUP_CH)
        return jnp.einsum("jgph,gk->jghkp", bb, eye).reshape(nb, S5_BUNDLE_CH, S5_BUNDLE_STATE)

    def pack_c(c):
        c = c.astype(F32).reshape(nb, S5_BUNDLE_GROUPS, S5_GROUP_CH, S5_STATE)
        return jnp.einsum("jghp,gk->jkpgh", c, eye).reshape(nb, S5_BUNDLE_STATE, S5_BUNDLE_CH)

    def pack_a(re, im):
        return jnp.concatenate([re.reshape(nb, 1, S5_BUNDLE_STATE), im.reshape(nb, 1, S5_BUNDLE_STATE)], axis=-1)

    bmat = jnp.concatenate([pack_b(bb_re), pack_b(bb_im)], axis=-1).astype(BF16)
    cmat = jnp.concatenate([pack_c(c_re), -pack_c(c_im)], axis=1).astype(BF16)
    return bmat, cmat, pack_a(ab_re, ab_im), pack_a(at_re, at_im)


def s5_mixer(xin, lam_re, lam_im, log_dt, b_re, b_im, c_re, c_im, d, w_glu, b_glu, *, tc=512):
    bsz, seq, width = xin.shape
    tc = min(tc, seq)
    tsub = tc // S5_SEGMENTS
    nb = width // S5_BUNDLE_CH
    nchunk = seq // tc
    bmat, cmat, a, at = _s5_tables(lam_re, lam_im, log_dt, b_re, b_im, c_re, c_im, tsub)
    xp = xin.reshape(bsz, nchunk, S5_SEGMENTS, tsub, nb, S5_BUNDLE_CH)
    xp = jnp.transpose(xp, (0, 4, 1, 3, 2, 5)).reshape(bsz, nb, seq, S5_BUNDLE_CH)
    ns2 = 2 * S5_BUNDLE_STATE
    out = pl.pallas_call(
        functools.partial(_s5_kernel, tsub=tsub),
        grid=(bsz, nchunk),
        in_specs=[pl.BlockSpec((None, nb, tc, S5_BUNDLE_CH), lambda b, c: (b, 0, c, 0)),
                  pl.BlockSpec((nb, S5_BUNDLE_CH, ns2), lambda b, c: (0, 0, 0)),
                  pl.BlockSpec((nb, ns2, S5_BUNDLE_CH), lambda b, c: (0, 0, 0)),
                  pl.BlockSpec((nb, 1, ns2), lambda b, c: (0, 0, 0)),
                  pl.BlockSpec((nb, 1, ns2), lambda b, c: (0, 0, 0)),
                  pl.BlockSpec((nb, 1, S5_BUNDLE_CH), lambda b, c: (0, 0, 0)),
                  pl.BlockSpec((nb, S5_BUNDLE_CH, width), lambda b, c: (0, 0, 0)),
                  pl.BlockSpec((1, width), lambda b, c: (0, 0))],
        out_specs=pl.BlockSpec((None, tc, width), lambda b, c: (b, c, 0)),
        out_shape=jax.ShapeDtypeStruct((bsz, seq, width), BF16),
        scratch_shapes=[pltpu.VMEM((tc, ns2), F32),
                        pltpu.VMEM((nb, tc, S5_BUNDLE_CH), F32),
                        pltpu.VMEM((S5_SEGMENTS, ns2), F32),
                        pltpu.VMEM((S5_SEGMENTS, ns2), F32),
                        pltpu.VMEM((nb, 1, ns2), F32)],
        compiler_params=_params("parallel", "arbitrary"),
        name="s5_mixer",
    )(xp, bmat, cmat, a, at, d.reshape(nb, 1, S5_BUNDLE_CH).astype(F32),
      w_glu.astype(BF16).reshape(nb, S5_BUNDLE_CH, width), b_glu.reshape(1, width).astype(F32))
    out = out.reshape(bsz, nchunk, tsub, S5_SEGMENTS, width)
    return jnp.transpose(out, (0, 1, 3, 2, 4)).reshape(bsz, seq, width)


def _merge_kernel(ya_ref, yb_ref, yc_ref, ga_ref, gb_ref, gc_ref, w_ref, o_ref):
    acc = ga_ref[...].astype(F32) * jnp.dot(ya_ref[...], w_ref[0], preferred_element_type=F32)
    acc += gb_ref[...].astype(F32) * jnp.dot(yb_ref[...], w_ref[1], preferred_element_type=F32)
    acc += gc_ref[...].astype(F32) * jnp.dot(yc_ref[...], w_ref[2], preferred_element_type=F32)
    o_ref[...] = acc.astype(o_ref.dtype)


def merge_branches(ya, yb, yc, gates, gate_col, w_branch, *, tm=512):
    n, width = ya.shape
    d = w_branch.shape[2]
    assert gate_col % d == 0
    y_spec = pl.BlockSpec((tm, width), lambda i: (i, 0))

    def gate_spec(branch):
        return pl.BlockSpec((tm, d), lambda i: (i, gate_col // d + branch))

    return pl.pallas_call(
        _merge_kernel,
        grid=(n // tm,),
        in_specs=[y_spec, y_spec, y_spec, gate_spec(0), gate_spec(1), gate_spec(2),
                  pl.BlockSpec((3, width, d), lambda i: (0, 0, 0))],
        out_specs=pl.BlockSpec((tm, d), lambda i: (i, 0)),
        out_shape=jax.ShapeDtypeStruct((n, d), BF16),
        compiler_params=_params("parallel"),
        name="merge_branches",
    )(ya, yb, yc, gates, gates, gates, w_branch)


def kernel(x, norm1_g, w_in, b_gate, gm_norm_g, gm_w_s, gm_b_s, s5_lambda_re, s5_lambda_im, s5_log_dt,
           s5_b_re, s5_b_im, s5_c_re, s5_c_im, s5_d, s5_w_glu, s5_b_glu, w_branch, w_out, norm2_g,
           w_mlp_in, w_mlp_out, final_g):
    bsz, seq, d_model = x.shape
    depth = w_in.shape[0]
    width = gm_norm_g.shape[1]
    heads = width // SB_HEAD_DIM
    o_a = 2 * width
    o_b = o_a + width
    o_c = o_b + 3 * width
    n = bsz * seq
    xf = x.reshape(n, d_model)
    col = jnp.arange(w_in.shape[2])
    col_scale = jnp.where((col >= o_b) & (col < o_b + width), SB_LOGIT_SCALE, 1.0).astype(F32)
    for l in range(depth):
        gate_bias = jnp.concatenate([jnp.zeros((o_c,), F32), b_gate[l].reshape(-1).astype(F32)])
        proj = matmul(xf, w_in, l, norm_g=norm1_g[l], scale=col_scale, bias=gate_bias, act="sigmoid",
                      act_from_col=o_c)
        y_a = gmlp(proj, gm_norm_g[l], gm_w_s[l], gm_b_s[l], width=width)
        proj3 = proj.reshape(bsz, seq, proj.shape[1])
        y_b = s5_mixer(proj3[:, :, o_a:o_b], s5_lambda_re[l], s5_lambda_im[l], s5_log_dt[l],
                       s5_b_re[l], s5_b_im[l], s5_c_re[l], s5_c_im[l], s5_d[l], s5_w_glu[l], s5_b_glu[l])
        y_c = sb_attention(proj3, o_b, o_b + width, o_b + 2 * width, heads=heads)
        merged = merge_branches(y_a, y_b.reshape(n, width), y_c.reshape(n, width), proj, o_c,
                                w_branch[l].astype(BF16))
        xf = matmul(merged, w_out, l, residual=xf, out_dtype=F32)
        act = matmul(xf, w_mlp_in, l, norm_g=norm2_g[l], act="relu2")
        xf = matmul(act, w_mlp_out, l, residual=xf, out_dtype=F32, tn=1024, tk=1024)
    return rmsnorm(xf, final_g, x.dtype).reshape(bsz, seq, d_model)
```

```python
import functools
import math

import jax
import jax.numpy as jnp
from jax import lax
from jax.experimental import pallas as pl
from jax.experimental.pallas import tpu as pltpu

F32 = jnp.float32
BF16 = jnp.bfloat16

EPS = 1e-6
LANES = 128
SUBLANES = 8
VMEM_LIMIT_BYTES = 60 * 1024 * 1024

GM_CHUNK = 128
GM_GROUP_CH = 128
S5_GROUP_CH = 16
S5_STATE = 64
S5_BUNDLE_GROUPS = 8
S5_BUNDLE_CH = S5_BUNDLE_GROUPS * S5_GROUP_CH
S5_BUNDLE_STATE = S5_BUNDLE_GROUPS * S5_STATE
S5_SEGMENTS = SUBLANES
SB_HEAD_DIM = 128


def _params(*semantics):
    return pltpu.CompilerParams(dimension_semantics=semantics, vmem_limit_bytes=VMEM_LIMIT_BYTES)


def _rmsnorm_kernel(x_ref, g_ref, o_ref):
    x = x_ref[...]
    y = x * lax.rsqrt(jnp.mean(x * x, axis=-1, keepdims=True) + EPS)
    o_ref[...] = (y * g_ref[...]).astype(o_ref.dtype)


def rmsnorm(x, g, out_dtype, tm=512):
    n, d = x.shape
    tm = min(tm, n)
    return pl.pallas_call(
        _rmsnorm_kernel,
        grid=(n // tm,),
        in_specs=[pl.BlockSpec((tm, d), lambda i: (i, 0)),
                  pl.BlockSpec((1, d), lambda i: (0, 0))],
        out_specs=pl.BlockSpec((tm, d), lambda i: (i, 0)),
        out_shape=jax.ShapeDtypeStruct((n, d), out_dtype),
        compiler_params=_params("parallel"),
        name="rmsnorm",
    )(x, g.reshape(1, d).astype(F32))


def _matmul_kernel(*refs, nk, act, act_from, has_norm, has_affine, has_res):
    refs = list(refs)
    x_ref = refs.pop(0)
    g_ref = refs.pop(0) if has_norm else None
    w_ref = refs.pop(0)
    s_ref, b_ref = (refs.pop(0), refs.pop(0)) if has_affine else (None, None)
    r_ref = refs.pop(0) if has_res else None
    o_ref = refs.pop(0)
    h_ref = refs.pop(0) if has_norm else None
    acc_ref = refs.pop(0) if nk > 1 else None
    j = pl.program_id(1)

    if has_norm:
        @pl.when(j == 0)
        def _():
            x = x_ref[...]
            y = x * lax.rsqrt(jnp.mean(x * x, axis=-1, keepdims=True) + EPS)
            h_ref[...] = (y * g_ref[...]).astype(BF16)

        lhs = h_ref[...]
    else:
        lhs = x_ref[...]

    def finish(acc):
        if has_affine:
            acc = acc * s_ref[...] + b_ref[...]

        def store(val):
            if has_res:
                val = val + r_ref[...]
            o_ref[...] = val.astype(o_ref.dtype)

        def activated():
            if act == "sigmoid":
                return jax.nn.sigmoid(acc)
            return jnp.square(jnp.maximum(acc, 0.0))

        if act is None:
            store(acc)
        elif act_from == 0:
            store(activated())
        else:
            pl.when(j >= act_from)(lambda: store(activated()))
            pl.when(j < act_from)(lambda: store(acc))

    part = jnp.dot(lhs, w_ref[...].astype(BF16), preferred_element_type=F32)
    if nk == 1:
        finish(part)
        return
    k = pl.program_id(2)

    @pl.when(k == 0)
    def _():
        acc_ref[...] = part

    @pl.when(k > 0)
    def _():
        acc_ref[...] += part

    @pl.when(k == nk - 1)
    def _():
        finish(acc_ref[...])


def matmul(x, w, layer, *, norm_g=None, scale=None, bias=None, act=None, act_from_col=0, residual=None,
           out_dtype=BF16, tm=2048, tn=512, tk=2048):
    m, kdim = x.shape
    n = w.shape[2]
    tm, tn, tk = min(tm, m), min(tn, n), min(tk, kdim)
    nk = kdim // tk
    has_norm = norm_g is not None
    has_affine = scale is not None or bias is not None
    assert not has_norm or nk == 1, "the fused rmsnorm needs whole rows"
    assert act_from_col % tn == 0
    if has_norm:
        in_specs = [pl.BlockSpec((tm, tk), lambda i, j, k: (i, k), pipeline_mode=pl.Buffered(1))]
    else:
        in_specs = [pl.BlockSpec((tm, tk), lambda i, j, k: (i, k))]
    args = [x]
    if has_norm:
        in_specs.append(pl.BlockSpec((1, kdim), lambda i, j, k: (0, 0)))
        args.append(norm_g.reshape(1, kdim).astype(F32))
    in_specs.append(pl.BlockSpec((None, tk, tn), lambda i, j, k: (layer, k, j)))
    args.append(w)
    if has_affine:
        row_spec = pl.BlockSpec((1, tn), lambda i, j, k: (0, j))
        in_specs += [row_spec, row_spec]
        args.append((jnp.ones((n,), F32) if scale is None else scale.astype(F32)).reshape(1, n))
        args.append((jnp.zeros((n,), F32) if bias is None else bias.astype(F32)).reshape(1, n))
    if residual is not None:
        if nk > 1:
            in_specs.append(pl.BlockSpec((tm, tn), lambda i, j, k: (i, j), pipeline_mode=pl.Buffered(1)))
        else:
            in_specs.append(pl.BlockSpec((tm, tn), lambda i, j, k: (i, j)))
        args.append(residual)
    scratch = []
    if has_norm:
        scratch.append(pltpu.VMEM((tm, kdim), BF16))
    if nk > 1:
        scratch.append(pltpu.VMEM((tm, tn), F32))
    kern = functools.partial(_matmul_kernel, nk=nk, act=act, act_from=act_from_col // tn, has_norm=has_norm,
                             has_affine=has_affine, has_res=residual is not None)
    return pl.pallas_call(
        kern,
        grid=(m // tm, n // tn, nk),
        in_specs=in_specs,
        out_specs=pl.BlockSpec((tm, tn), lambda i, j, k: (i, j)),
        out_shape=jax.ShapeDtypeStruct((m, n), out_dtype),
        scratch_shapes=scratch,
        compiler_params=_params("parallel", "arbitrary", "arbitrary"),
        name="matmul",
    )(*args)


def _gmlp_kernel(uv_ref, g_ref, w_ref, bt_ref, o_ref, *, width):
    z = jax.nn.gelu(uv_ref[...].astype(F32))
    u = z[:, :width]
    v = z[:, width:]
    v = v * lax.rsqrt(jnp.mean(v * v, axis=-1, keepdims=True) + EPS) * g_ref[...]
    v = v.astype(BF16)
    tm = u.shape[0]
    t_idx = lax.broadcasted_iota(jnp.int32, (GM_CHUNK, GM_CHUNK), 0)
    s_idx = lax.broadcasted_iota(jnp.int32, (GM_CHUNK, GM_CHUNK), 1)
    causal = s_idx <= t_idx
    for g in range(width // GM_GROUP_CH):
        w = jnp.where(causal, w_ref[g], 0.0).astype(BF16)
        bias = bt_ref[:, g:g + 1]
        cols = slice(g * GM_GROUP_CH, (g + 1) * GM_GROUP_CH)
        for c in range(tm // GM_CHUNK):
            rows = slice(c * GM_CHUNK, (c + 1) * GM_CHUNK)
            mixed = jnp.dot(w, v[rows, cols], preferred_element_type=F32) + bias
            o_ref[rows, cols] = (u[rows, cols] * mixed).astype(o_ref.dtype)


def gmlp(proj, norm_g, w_s, b_s, *, width, tm=256):
    n = proj.shape[0]
    groups = width // GM_GROUP_CH
    return pl.pallas_call(
        functools.partial(_gmlp_kernel, width=width),
        grid=(n // tm,),
        in_specs=[pl.BlockSpec((tm, 2 * width), lambda i: (i, 0)),
                  pl.BlockSpec((1, width), lambda i: (0, 0)),
                  pl.BlockSpec((groups, GM_CHUNK, GM_CHUNK), lambda i: (0, 0, 0)),
                  pl.BlockSpec((GM_CHUNK, groups), lambda i: (0, 0))],
        out_specs=pl.BlockSpec((tm, width), lambda i: (i, 0)),
        out_shape=jax.ShapeDtypeStruct((n, width), BF16),
        compiler_params=_params("parallel"),
        name="gmlp",
    )(proj, norm_g.reshape(1, width).astype(F32), w_s.astype(F32), jnp.transpose(b_s).astype(F32))


SB_LOGIT_SCALE = SB_HEAD_DIM ** -0.5 * math.log2(math.e)


SB_MASKED_LOGIT = -1e30


def _sb_attn_kernel(q_ref, k_ref, v_ref, tri_ref, o_ref, z_scr, p_scr, tot_scr, *, tq, blk, hps):
    qi = pl.program_id(2)
    dh = SB_HEAD_DIM
    tri = tri_ref[...]
    nblk = tq // blk

    def logits(h, start, c, diagonal):
        hc = slice(h * dh, (h + 1) * dh)
        k = k_ref[pl.ds(pl.multiple_of(start + c * blk, blk), blk), hc]
        z = lax.dot_general(q_ref[:, hc], k, (((1,), (1,)), ((), ())), preferred_element_type=F32)
        if diagonal:
            t_idx = lax.broadcasted_iota(jnp.int32, (tq, blk), 0)
            s_idx = c * blk + lax.broadcasted_iota(jnp.int32, (tq, blk), 1)
            z = jnp.where(s_idx < t_idx, z, SB_MASKED_LOGIT)
        neg_abs = pltpu.bitcast(pltpu.bitcast(z, jnp.uint32) | jnp.uint32(0x80000000), F32)
        p = jnp.maximum(z, 0.0) + jnp.log2(1.0 + jnp.exp2(neg_abs))
        return z - p, p.astype(BF16), jnp.sum(p, axis=1, keepdims=True)

    def stash(h, c, vals):
        z_scr[h, c], p_scr[h, c], tot_scr[h, c] = vals

    def weights_step(h, start, c, run, acc):
        hc = slice(h * dh, (h + 1) * dh)
        cs = jnp.dot(p_scr[h, c], tri, preferred_element_type=F32)
        w = jnp.exp2(z_scr[h, c] - cs - run).astype(BF16)
        v = v_ref[pl.ds(pl.multiple_of(start + c * blk, blk), blk), hc]
        return run + tot_scr[h, c], acc + jnp.dot(w, v, preferred_element_type=F32)

    for h in range(hps):
        for c in reversed(range(nblk)):
            stash(h, c, logits(h, qi * tq, c, True))

    def trip(i, carry):
        cur = (qi - i) * tq
        out = []
        for h in range(hps):
            run, acc = carry[h]
            for c in reversed(range(nblk)):
                run, acc = weights_step(h, cur, c, run, acc)
                stash(h, c, logits(h, cur - tq, c, False))
            out.append((run, acc))
        return tuple(out)

    carry = tuple((jnp.zeros((tq, 1), F32), jnp.zeros((tq, dh), F32)) for _ in range(hps))
    carry = lax.fori_loop(0, qi, trip, carry)
    for h in range(hps):
        run, acc = carry[h]
        for c in reversed(range(nblk)):
            run, acc = weights_step(h, 0, c, run, acc)
        o_ref[:, h * dh:(h + 1) * dh] = acc.astype(o_ref.dtype)


def sb_attention(proj, q_col, k_col, v_col, *, heads, tq=512, blk=256, hps=2):
    bsz, seq, _ = proj.shape
    hw = hps * SB_HEAD_DIM
    tq = min(tq, seq)
    j_idx = jnp.arange(blk)
    tri = (j_idx[:, None] > j_idx[None, :]).astype(BF16)
    kern = functools.partial(_sb_attn_kernel, tq=tq, blk=blk, hps=hps)
    return pl.pallas_call(
        kern,
        grid=(bsz, heads // hps, seq // tq),
        in_specs=[pl.BlockSpec((None, tq, hw), lambda b, h, i: (b, i, q_col // hw + h)),
                  pl.BlockSpec((None, seq, hw), lambda b, h, i: (b, 0, k_col // hw + h)),
                  pl.BlockSpec((None, seq, hw), lambda b, h, i: (b, 0, v_col // hw + h)),
                  pl.BlockSpec((blk, blk), lambda b, h, i: (0, 0))],
        out_specs=pl.BlockSpec((None, tq, hw), lambda b, h, i: (b, i, h)),
        out_shape=jax.ShapeDtypeStruct((bsz, seq, heads * SB_HEAD_DIM), BF16),
        scratch_shapes=[pltpu.VMEM((hps, tq // blk, tq, blk), F32),
                        pltpu.VMEM((hps, tq // blk, tq, blk), BF16),
                        pltpu.VMEM((hps, tq // blk, tq, 1), F32)],
        compiler_params=_params("parallel", "parallel", "arbitrary"),
        name="sb_attention",
    )(proj, proj, proj, tri)


def _s5_kernel(x_ref, bmat_ref, cmat_ref, a_ref, at_ref, d_ref, wglu_ref, bglu_ref, o_ref,
               bu_ref, g_ref, fin_ref, sin_ref, carry_ref, *, tsub):
    nb = x_ref.shape[0]
    ns = S5_BUNDLE_STATE
    tc = tsub * S5_SEGMENTS

    @pl.when(pl.program_id(1) == 0)
    def _():
        carry_ref[...] = jnp.zeros_like(carry_ref)

    def cmul_add(ar, ai, sr, si, br, bi):
        return ar * sr - ai * si + br, ar * si + ai * sr + bi

    def b_matmul(j):
        bu_ref[j % 2] = jnp.dot(x_ref[j], bmat_ref[j], preferred_element_type=F32)

    def scan(j):
        slot = j % 2
        a = a_ref[j]
        ar = jnp.broadcast_to(a[:, :ns], (S5_SEGMENTS, ns))
        ai = jnp.broadcast_to(a[:, ns:], (S5_SEGMENTS, ns))

        def rows(i):
            return slice(i * S5_SEGMENTS, (i + 1) * S5_SEGMENTS)

        sr = si = jnp.zeros((S5_SEGMENTS, ns), F32)
        for i in range(tsub):
            sr, si = cmul_add(ar, ai, sr, si, bu_ref[slot, rows(i), :ns], bu_ref[slot, rows(i), ns:])
        fin_ref[j, :, :ns] = sr
        fin_ref[j, :, ns:] = si
        at = at_ref[j]
        atr, ati = at[:, :ns], at[:, ns:]
        pr = carry_ref[j, :, :ns]
        pi = carry_ref[j, :, ns:]
        for k in range(S5_SEGMENTS):
            sin_ref[j, k:k + 1, :ns] = pr
            sin_ref[j, k:k + 1, ns:] = pi
            pr, pi = cmul_add(atr, ati, pr, pi, fin_ref[j, k:k + 1, :ns], fin_ref[j, k:k + 1, ns:])
        carry_ref[j, :, :ns] = pr
        carry_ref[j, :, ns:] = pi
        sr, si = sin_ref[j, :, :ns], sin_ref[j, :, ns:]
        for i in range(tsub):
            sr, si = cmul_add(ar, ai, sr, si, bu_ref[slot, rows(i), :ns], bu_ref[slot, rows(i), ns:])
            bu_ref[slot, rows(i), :ns] = sr
            bu_ref[slot, rows(i), ns:] = si

    def c_matmul(j):
        y = jnp.dot(bu_ref[j % 2].astype(BF16), cmat_ref[j], preferred_element_type=F32)
        y = y + d_ref[j] * x_ref[j].astype(F32)
        g_ref[j] = jax.nn.gelu(y)

    b_matmul(0)
    for j in range(nb):
        if j + 1 < nb:
            b_matmul(j + 1)
        scan(j)
        c_matmul(j)

    gate = jnp.zeros((tc, nb * S5_BUNDLE_CH), F32)
    for j in range(nb):
        gate = gate + jnp.dot(g_ref[j].astype(BF16), wglu_ref[j], preferred_element_type=F32)
    gate = jax.nn.sigmoid(gate + bglu_ref[...])
    for j in range(nb):
        cols = slice(j * S5_BUNDLE_CH, (j + 1) * S5_BUNDLE_CH)
        o_ref[:, cols] = (g_ref[j] * gate[:, cols]).astype(o_ref.dtype)


def _s5_tables(lam_re, lam_im, log_dt, b_re, b_im, c_re, c_im, tsub):
    groups = lam_re.shape[0]
    nb = groups // S5_BUNDLE_GROUPS
    dt = jnp.exp(log_dt.astype(F32))[:, None]
    lam_re = lam_re.astype(F32)
    lam_im = lam_im.astype(F32)
    mag = jnp.exp(lam_re * dt)
    ab_re = mag * jnp.cos(lam_im * dt)
    ab_im = mag * jnp.sin(lam_im * dt)
    den = lam_re * lam_re + lam_im * lam_im
    n_re = ab_re - 1.0
    n_im = ab_im
    k_re = (n_re * lam_re + n_im * lam_im) / den
    k_im = (n_im * lam_re - n_re * lam_im) / den
    b_re = b_re.astype(F32)
    b_im = b_im.astype(F32)
    bb_re = k_re[..., None] * b_re - k_im[..., None] * b_im
    bb_im = k_re[..., None] * b_im + k_im[..., None] * b_re
    magt = jnp.exp(lam_re * dt * tsub)
    at_re = magt * jnp.cos(lam_im * dt * tsub)
    at_im = magt * jnp.sin(lam_im * dt * tsub)
    eye = jnp.eye(S5_BUNDLE_GROUPS, dtype=F32)

    def pack_b(bb):
        bb = bb.reshape(nb, S5_BUNDLE_GROUPS, S5_STATE, S5_GROUP_CH)
        return jnp.einsum("jgph,gk->jghkp", bb, eye).reshape(nb, S5_BUNDLE_CH, S5_BUNDLE_STATE)

    def pack_c(c):
        c = c.astype(F32).reshape(nb, S5_BUNDLE_GROUPS, S5_GROUP_CH, S5_STATE)
        return jnp.einsum("jghp,gk->jkpgh", c, eye).reshape(nb, S5_BUNDLE_STATE, S5_BUNDLE_CH)

    def pack_a(re, im):
        return jnp.concatenate([re.reshape(nb, 1, S5_BUNDLE_STATE), im.reshape(nb, 1, S5_BUNDLE_STATE)], axis=-1)

    bmat = jnp.concatenate([pack_b(bb_re), pack_b(bb_im)], axis=-1).astype(BF16)
    cmat = jnp.concatenate([pack_c(c_re), -pack_c(c_im)], axis=1).astype(BF16)
    return bmat, cmat, pack_a(ab_re, ab_im), pack_a(at_re, at_im)


def s5_mixer(xin, lam_re, lam_im, log_dt, b_re, b_im, c_re, c_im, d, w_glu, b_glu, *, tc=512):
    bsz, seq, width = xin.shape
    tc = min(tc, seq)
    tsub = tc // S5_SEGMENTS
    nb = width // S5_BUNDLE_CH
    nchunk = seq // tc
    bmat, cmat, a, at = _s5_tables(lam_re, lam_im, log_dt, b_re, b_im, c_re, c_im, tsub)
    xp = xin.reshape(bsz, nchunk, S5_SEGMENTS, tsub, nb, S5_BUNDLE_CH)
    xp = jnp.transpose(xp, (0, 4, 1, 3, 2, 5)).reshape(bsz, nb, seq, S5_BUNDLE_CH)
    ns2 = 2 * S5_BUNDLE_STATE
    out = pl.pallas_call(
        functools.partial(_s5_kernel, tsub=tsub),
        grid=(bsz, nchunk),
        in_specs=[pl.BlockSpec((None, nb, tc, S5_BUNDLE_CH), lambda b, c: (b, 0, c, 0)),
                  pl.BlockSpec((nb, S5_BUNDLE_CH, ns2), lambda b, c: (0, 0, 0)),
                  pl.BlockSpec((nb, ns2, S5_BUNDLE_CH), lambda b, c: (0, 0, 0)),
                  pl.BlockSpec((nb, 1, ns2), lambda b, c: (0, 0, 0)),
                  pl.BlockSpec((nb, 1, ns2), lambda b, c: (0, 0, 0)),
                  pl.BlockSpec((nb, 1, S5_BUNDLE_CH), lambda b, c: (0, 0, 0)),
                  pl.BlockSpec((nb, S5_BUNDLE_CH, width), lambda b, c: (0, 0, 0)),
                  pl.BlockSpec((1, width), lambda b, c: (0, 0))],
        out_specs=pl.BlockSpec((None, tc, width), lambda b, c: (b, c, 0)),
        out_shape=jax.ShapeDtypeStruct((bsz, seq, width), BF16),
        scratch_shapes=[pltpu.VMEM((2, tc, ns2), F32),
                        pltpu.VMEM((nb, tc, S5_BUNDLE_CH), F32),
                        pltpu.VMEM((nb, S5_SEGMENTS, ns2), F32),
                        pltpu.VMEM((nb, S5_SEGMENTS, ns2), F32),
                        pltpu.VMEM((nb, 1, ns2), F32)],
        compiler_params=_params("parallel", "arbitrary"),
        name="s5_mixer",
    )(xp, bmat, cmat, a, at, d.reshape(nb, 1, S5_BUNDLE_CH).astype(F32),
      w_glu.astype(BF16).reshape(nb, S5_BUNDLE_CH, width), b_glu.reshape(1, width).astype(F32))
    out = out.reshape(bsz, nchunk, tsub, S5_SEGMENTS, width)
    return jnp.transpose(out, (0, 1, 3, 2, 4)).reshape(bsz, seq, width)


def _merge_kernel(ya_ref, yb_ref, yc_ref, ga_ref, gb_ref, gc_ref, w_ref, o_ref):
    acc = ga_ref[...].astype(F32) * jnp.dot(ya_ref[...], w_ref[0], preferred_element_type=F32)
    acc += gb_ref[...].astype(F32) * jnp.dot(yb_ref[...], w_ref[1], preferred_element_type=F32)
    acc += gc_ref[...].astype(F32) * jnp.dot(yc_ref[...], w_ref[2], preferred_element_type=F32)
    o_ref[...] = acc.astype(o_ref.dtype)


def merge_branches(ya, yb, yc, gates, gate_col, w_branch, *, tm=512):
    n, width = ya.shape
    d = w_branch.shape[2]
    assert gate_col % d == 0
    y_spec = pl.BlockSpec((tm, width), lambda i: (i, 0))

    def gate_spec(branch):
        return pl.BlockSpec((tm, d), lambda i: (i, gate_col // d + branch))

    return pl.pallas_call(
        _merge_kernel,
        grid=(n // tm,),
        in_specs=[y_spec, y_spec, y_spec, gate_spec(0), gate_spec(1), gate_spec(2),
                  pl.BlockSpec((3, width, d), lambda i: (0, 0, 0))],
        out_specs=pl.BlockSpec((tm, d), lambda i: (i, 0)),
        out_shape=jax.ShapeDtypeStruct((n, d), BF16),
        compiler_params=_params("parallel"),
        name="merge_branches",
    )(ya, yb, yc, gates, gates, gates, w_branch)


def kernel(x, norm1_g, w_in, b_gate, gm_norm_g, gm_w_s, gm_b_s, s5_lambda_re, s5_lambda_im, s5_log_dt,
           s5_b_re, s5_b_im, s5_c_re, s5_c_im, s5_d, s5_w_glu, s5_b_glu, w_branch, w_out, norm2_g,
           w_mlp_in, w_mlp_out, final_g):
    bsz, seq, d_model = x.shape
    depth = w_in.shape[0]
    width = gm_norm_g.shape[1]
    heads = width // SB_HEAD_DIM
    o_a = 2 * width
    o_b = o_a + width
    o_c = o_b + 3 * width
    n = bsz * seq
    xf = x.reshape(n, d_model)
    col = jnp.arange(w_in.shape[2])
    col_scale = jnp.where((col >= o_b) & (col < o_b + width), SB_LOGIT_SCALE, 1.0).astype(F32)
    for l in range(depth):
        gate_bias = jnp.concatenate([jnp.zeros((o_c,), F32), b_gate[l].reshape(-1).astype(F32)])
        proj = matmul(xf, w_in, l, norm_g=norm1_g[l], scale=col_scale, bias=gate_bias, act="sigmoid",
                      act_from_col=o_c)
        y_a = gmlp(proj, gm_norm_g[l], gm_w_s[l], gm_b_s[l], width=width)
        proj3 = proj.reshape(bsz, seq, proj.shape[1])
        y_b = s5_mixer(proj3[:, :, o_a:o_b], s5_lambda_re[l], s5_lambda_im[l], s5_log_dt[l],
                       s5_b_re[l], s5_b_im[l], s5_c_re[l], s5_c_im[l], s5_d[l], s5_w_glu[l], s5_b_glu[l])
        y_c = sb_attention(proj3, o_b, o_b + width, o_b + 2 * width, heads=heads)
        merged = merge_branches(y_a, y_b.reshape(n, width), y_c.reshape(n, width), proj, o_c,
                                w_branch[l].astype(BF16))
        xf = matmul(merged, w_out, l, residual=xf, out_dtype=F32)
        act = matmul(xf, w_mlp_in, l, norm_g=norm2_g[l], act="relu2")
        xf = matmul(act, w_mlp_out, l, residual=xf, out_dtype=F32, tn=1024, tk=1024)
    return rmsnorm(xf, final_g, x.dtype).reshape(bsz, seq, d_model)
```

```python
import functools
import math

import jax
import jax.numpy as jnp
from jax import lax
from jax.experimental import pallas as pl
from jax.experimental.pallas import tpu as pltpu

F32 = jnp.float32
BF16 = jnp.bfloat16

EPS = 1e-6
LANES = 128
SUBLANES = 8
MXU_WIDTH = 256
VMEM_LIMIT_BYTES = 60 * 1024 * 1024

GM_CHUNK = 128
GM_GROUP_CH = 128
S5_GROUP_CH = 16
S5_STATE = 64
S5_BUNDLE_GROUPS = 8
S5_BUNDLE_CH = S5_BUNDLE_GROUPS * S5_GROUP_CH
S5_BUNDLE_STATE = S5_BUNDLE_GROUPS * S5_STATE
S5_SEGMENTS = SUBLANES
SB_HEAD_DIM = 128


def _params(*semantics):
    return pltpu.CompilerParams(dimension_semantics=semantics, vmem_limit_bytes=VMEM_LIMIT_BYTES)


def _rmsnorm_kernel(x_ref, g_ref, o_ref):
    x = x_ref[...]
    y = x * lax.rsqrt(jnp.mean(x * x, axis=-1, keepdims=True) + EPS)
    o_ref[...] = (y * g_ref[...]).astype(o_ref.dtype)


def rmsnorm(x, g, out_dtype, tm=512):
    n, d = x.shape
    tm = min(tm, n)
    return pl.pallas_call(
        _rmsnorm_kernel,
        grid=(n // tm,),
        in_specs=[pl.BlockSpec((tm, d), lambda i: (i, 0)),
                  pl.BlockSpec((1, d), lambda i: (0, 0))],
        out_specs=pl.BlockSpec((tm, d), lambda i: (i, 0)),
        out_shape=jax.ShapeDtypeStruct((n, d), out_dtype),
        compiler_params=_params("parallel"),
        name="rmsnorm",
    )(x, g.reshape(1, d).astype(F32))


def _matmul_kernel(*refs, nk, act, act_from, has_norm, has_affine, has_res):
    refs = list(refs)
    x_ref = refs.pop(0)
    g_ref = refs.pop(0) if has_norm else None
    w_ref = refs.pop(0)
    s_ref, b_ref = (refs.pop(0), refs.pop(0)) if has_affine else (None, None)
    r_ref = refs.pop(0) if has_res else None
    o_ref = refs.pop(0)
    h_ref = refs.pop(0) if has_norm else None
    j = pl.program_id(1)

    if has_norm:
        @pl.when(j == 0)
        def _():
            x = x_ref[...]
            y = x * lax.rsqrt(jnp.mean(x * x, axis=-1, keepdims=True) + EPS)
            h_ref[...] = (y * g_ref[...]).astype(BF16)

        lhs_ref = h_ref
    else:
        lhs_ref = x_ref

    if nk > 1:
        @pl.when(pl.program_id(2) == 0)
        def _():
            o_ref[...] = r_ref[...] if has_res else jnp.zeros_like(o_ref)

    for c in range(o_ref.shape[1] // MXU_WIDTH):
        cols = slice(c * MXU_WIDTH, (c + 1) * MXU_WIDTH)
        acc = jnp.dot(lhs_ref[...], w_ref[:, cols].astype(BF16), preferred_element_type=F32)
        if nk > 1:
            o_ref[:, cols] += acc
            continue
        if has_affine:
            acc = acc * s_ref[:, cols] + b_ref[:, cols]
        if act is not None:
            activated = jax.nn.sigmoid(acc) if act == "sigmoid" else jnp.square(jnp.maximum(acc, 0.0))
            acc = activated if act_from == 0 else jnp.where(j >= act_from, activated, acc)
        if has_res:
            acc = acc + r_ref[:, cols]
        o_ref[:, cols] = acc.astype(o_ref.dtype)


def matmul(x, w, layer, *, norm_g=None, scale=None, bias=None, act=None, act_from_col=0, residual=None,
           out_dtype=BF16, tm=2048, tn=512, tk=2048):
    m, kdim = x.shape
    n = w.shape[2]
    tm, tn, tk = min(tm, m), min(tn, n), min(tk, kdim)
    nk = kdim // tk
    has_norm = norm_g is not None
    has_affine = scale is not None or bias is not None
    assert not has_norm or nk == 1, "the fused rmsnorm needs whole rows"
    assert nk == 1 or (out_dtype == F32 and act is None and not has_affine), "k-split accumulates in the output"
    assert act_from_col % tn == 0
    if has_norm:
        in_specs = [pl.BlockSpec((tm, tk), lambda i, j, k: (i, k), pipeline_mode=pl.Buffered(1))]
    else:
        in_specs = [pl.BlockSpec((tm, tk), lambda i, j, k: (i, k))]
    args = [x]
    if has_norm:
        in_specs.append(pl.BlockSpec((1, kdim), lambda i, j, k: (0, 0)))
        args.append(norm_g.reshape(1, kdim).astype(F32))
    in_specs.append(pl.BlockSpec((None, tk, tn), lambda i, j, k: (layer, k, j)))
    args.append(w)
    if has_affine:
        row_spec = pl.BlockSpec((1, tn), lambda i, j, k: (0, j))
        in_specs += [row_spec, row_spec]
        args.append((jnp.ones((n,), F32) if scale is None else scale.astype(F32)).reshape(1, n))
        args.append((jnp.zeros((n,), F32) if bias is None else bias.astype(F32)).reshape(1, n))
    if residual is not None:
        if nk > 1:
            in_specs.append(pl.BlockSpec((tm, tn), lambda i, j, k: (i, j), pipeline_mode=pl.Buffered(1)))
        else:
            in_specs.append(pl.BlockSpec((tm, tn), lambda i, j, k: (i, j)))
        args.append(residual)
    scratch = []
    if has_norm:
        scratch.append(pltpu.VMEM((tm, kdim), BF16))
    kern =functools.partial(_matmul_kernel, nk=nk, act=act, act_from=act_from_col // tn, has_norm=has_norm,
                             has_affine=has_affine, has_res=residual is not None)
    return pl.pallas_call(
        kern,
        grid=(m // tm, n // tn, nk),
        in_specs=in_specs,
        out_specs=pl.BlockSpec((tm, tn), lambda i, j, k: (i, j)),
        out_shape=jax.ShapeDtypeStruct((m, n), out_dtype),
        scratch_shapes=scratch,
        compiler_params=_params("parallel", "arbitrary", "arbitrary"),
        name="matmul",
    )(*args)


def _gmlp_kernel(uv_ref, g_ref, w_ref, bt_ref, o_ref, *, width):
    z = jax.nn.gelu(uv_ref[...].astype(F32))
    u = z[:, :width]
    v = z[:, width:]
    v = v * lax.rsqrt(jnp.mean(v * v, axis=-1, keepdims=True) + EPS) * g_ref[...]
    v = v.astype(BF16)
    tm = u.shape[0]
    t_idx = lax.broadcasted_iota(jnp.int32, (GM_CHUNK, GM_CHUNK), 0)
    s_idx = lax.broadcasted_iota(jnp.int32, (GM_CHUNK, GM_CHUNK), 1)
    causal = s_idx <= t_idx
    for g in range(width // GM_GROUP_CH):
        w = jnp.where(causal, w_ref[g], 0.0).astype(BF16)
        bias = bt_ref[:, g:g + 1]
        cols = slice(g * GM_GROUP_CH, (g + 1) * GM_GROUP_CH)
        for c in range(tm // GM_CHUNK):
            rows = slice(c * GM_CHUNK, (c + 1) * GM_CHUNK)
            mixed = jnp.dot(w, v[rows, cols], preferred_element_type=F32) + bias
            o_ref[rows, cols] = (u[rows, cols] * mixed).astype(o_ref.dtype)


def gmlp(proj, norm_g, w_s, b_s, *, width, tm=256):
    n = proj.shape[0]
    groups = width // GM_GROUP_CH
    return pl.pallas_call(
        functools.partial(_gmlp_kernel, width=width),
        grid=(n // tm,),
        in_specs=[pl.BlockSpec((tm, 2 * width), lambda i: (i, 0)),
                  pl.BlockSpec((1, width), lambda i: (0, 0)),
                  pl.BlockSpec((groups, GM_CHUNK, GM_CHUNK), lambda i: (0, 0, 0)),
                  pl.BlockSpec((GM_CHUNK, groups), lambda i: (0, 0))],
        out_specs=pl.BlockSpec((tm, width), lambda i: (i, 0)),
        out_shape=jax.ShapeDtypeStruct((n, width), BF16),
        compiler_params=_params("parallel"),
        name="gmlp",
    )(proj, norm_g.reshape(1, width).astype(F32), w_s.astype(F32), jnp.transpose(b_s).astype(F32))


SB_LOGIT_SCALE = SB_HEAD_DIM ** -0.5 * math.log2(math.e)


SB_MASKED_LOGIT = -1e30


def _sb_attn_kernel(q_ref, k_ref, v_ref, tri_ref, o_ref, z_scr, p_scr, tot_scr, *, tq, blk, hps):
    qi = pl.program_id(2)
    dh = SB_HEAD_DIM
    tri = tri_ref[...]
    nblk = tq // blk

    def logits(h, start, c, diagonal):
        hc = slice(h * dh, (h + 1) * dh)
        k = k_ref[pl.ds(pl.multiple_of(start + c * blk, blk), blk), hc]
        z = lax.dot_general(q_ref[:, hc], k, (((1,), (1,)), ((), ())), preferred_element_type=F32)
        if diagonal:
            t_idx = lax.broadcasted_iota(jnp.int32, (tq, blk), 0)
            s_idx = c * blk + lax.broadcasted_iota(jnp.int32, (tq, blk), 1)
            z = jnp.where(s_idx < t_idx, z, SB_MASKED_LOGIT)
        neg_abs = pltpu.bitcast(pltpu.bitcast(z, jnp.uint32) | jnp.uint32(0x80000000), F32)
        p = jnp.maximum(z, 0.0) + jnp.log2(1.0 + jnp.exp2(neg_abs))
        return z - p, p.astype(BF16), jnp.sum(p, axis=1, keepdims=True)

    def stash(h, c, vals):
        z_scr[h, c], p_scr[h, c], tot_scr[h, c] = vals

    def weights_step(h, start, c, run, acc):
        hc = slice(h * dh, (h + 1) * dh)
        cs = jnp.dot(p_scr[h, c], tri, preferred_element_type=F32)
        w = jnp.exp2(z_scr[h, c] - cs - run).astype(BF16)
        v = v_ref[pl.ds(pl.multiple_of(start + c * blk, blk), blk), hc]
        return run + tot_scr[h, c], acc + jnp.dot(w, v, preferred_element_type=F32)

    for h in range(hps):
        for c in reversed(range(nblk)):
            stash(h, c, logits(h, qi * tq, c, True))

    def trip(i, carry):
        cur = (qi - i) * tq
        out = []
        for h in range(hps):
            run, acc = carry[h]
            for c in reversed(range(nblk)):
                run, acc = weights_step(h, cur, c, run, acc)
                stash(h, c, logits(h, cur - tq, c, False))
            out.append((run, acc))
        return tuple(out)

    carry = tuple((jnp.zeros((tq, 1), F32), jnp.zeros((tq, dh), F32)) for _ in range(hps))
    carry = lax.fori_loop(0, qi, trip, carry)
    for h in range(hps):
        run, acc = carry[h]
        for c in reversed(range(nblk)):
            run, acc = weights_step(h, 0, c, run, acc)
        o_ref[:, h * dh:(h + 1) * dh] = acc.astype(o_ref.dtype)


def sb_attention(proj, q_col, k_col, v_col, *, heads, tq=512, blk=256, hps=2):
    bsz, seq, _ = proj.shape
    hw = hps * SB_HEAD_DIM
    tq = min(tq, seq)
    j_idx = jnp.arange(blk)
    tri = (j_idx[:, None] > j_idx[None, :]).astype(BF16)
    kern = functools.partial(_sb_attn_kernel, tq=tq, blk=blk, hps=hps)
    return pl.pallas_call(
        kern,
        grid=(bsz, heads // hps, seq // tq),
        in_specs=[pl.BlockSpec((None, tq, hw), lambda b, h, i: (b, i, q_col // hw + h)),
                  pl.BlockSpec((None, seq, hw), lambda b, h, i: (b, 0, k_col // hw + h)),
                  pl.BlockSpec((None, seq, hw), lambda b, h, i: (b, 0, v_col // hw + h)),
                  pl.BlockSpec((blk, blk), lambda b, h, i: (0, 0))],
        out_specs=pl.BlockSpec((None, tq, hw), lambda b, h, i: (b, i, h)),
        out_shape=jax.ShapeDtypeStruct((bsz, seq, heads * SB_HEAD_DIM), BF16),
        scratch_shapes=[pltpu.VMEM((hps, tq // blk, tq, blk), F32),
                        pltpu.VMEM((hps, tq // blk, tq, blk), BF16),
                        pltpu.VMEM((hps, tq // blk, tq, 1), F32)],
        compiler_params=_params("parallel", "parallel", "arbitrary"),
        name="sb_attention",
    )(proj, proj, proj, tri)


def _s5_kernel(x_ref, bmat_ref, cmat_ref, a_ref, at_ref, d_ref, wglu_ref, bglu_ref, o_ref,
               bu_ref, g_ref, fin_ref, sin_ref, carry_ref, *, tsub):
    nb = x_ref.shape[0]
    ns = S5_BUNDLE_STATE
    tc = tsub * S5_SEGMENTS

    @pl.when(pl.program_id(1) == 0)
    def _():
        carry_ref[...] = jnp.zeros_like(carry_ref)

    def cmul_add(ar, ai, sr, si, br, bi):
        return ar * sr - ai * si + br, ar * si + ai * sr + bi

    def b_matmul(j):
        bu_ref[j % 2] = jnp.dot(x_ref[j], bmat_ref[j], preferred_element_type=F32)

    def scan(j):
        slot = j % 2
        a = a_ref[j]
        ar = jnp.broadcast_to(a[:, :ns], (S5_SEGMENTS, ns))
        ai = jnp.broadcast_to(a[:, ns:], (S5_SEGMENTS, ns))

        def rows(i):
            return slice(i * S5_SEGMENTS, (i + 1) * S5_SEGMENTS)

        sr = si = jnp.zeros((S5_SEGMENTS, ns), F32)
        for i in range(tsub):
            sr, si = cmul_add(ar, ai, sr, si, bu_ref[slot, rows(i), :ns], bu_ref[slot, rows(i), ns:])
        fin_ref[j, :, :ns] = sr
        fin_ref[j, :, ns:] = si
        at = at_ref[j]
        atr, ati = at[:, :ns], at[:, ns:]
        pr = carry_ref[j, :, :ns]
        pi = carry_ref[j, :, ns:]
        for k in range(S5_SEGMENTS):
            sin_ref[j, k:k + 1, :ns] = pr
            sin_ref[j, k:k + 1, ns:] = pi
            pr, pi = cmul_add(atr, ati, pr, pi, fin_ref[j, k:k + 1, :ns], fin_ref[j, k:k + 1, ns:])
        carry_ref[j, :, :ns] = pr
        carry_ref[j, :, ns:] = pi
        sr, si = sin_ref[j, :, :ns], sin_ref[j, :, ns:]
        for i in range(tsub):
            sr, si = cmul_add(ar, ai, sr, si, bu_ref[slot, rows(i), :ns], bu_ref[slot, rows(i), ns:])
            bu_ref[slot, rows(i), :ns] = sr
            bu_ref[slot, rows(i), ns:] = si

    def c_matmul(j):
        y = jnp.dot(bu_ref[j % 2].astype(BF16), cmat_ref[j], preferred_element_type=F32)
        y = y + d_ref[j] * x_ref[j].astype(F32)
        g_ref[j] = jax.nn.gelu(y)

    b_matmul(0)
    for j in range(nb):
        if j + 1 < nb:
            b_matmul(j + 1)
        scan(j)
        c_matmul(j)

    gate = jnp.zeros((tc, nb * S5_BUNDLE_CH), F32)
    for j in range(nb):
        gate = gate + jnp.dot(g_ref[j].astype(BF16), wglu_ref[j], preferred_element_type=F32)
    gate = jax.nn.sigmoid(gate + bglu_ref[...])
    for j in range(nb):
        cols = slice(j * S5_BUNDLE_CH, (j + 1) * S5_BUNDLE_CH)
        o_ref[:, cols] = (g_ref[j] * gate[:, cols]).astype(o_ref.dtype)


def _s5_tables(lam_re, lam_im, log_dt, b_re, b_im, c_re, c_im, tsub):
    groups = lam_re.shape[0]
    nb = groups // S5_BUNDLE_GROUPS
    dt = jnp.exp(log_dt.astype(F32))[:, None]
    lam_re = lam_re.astype(F32)
    lam_im = lam_im.astype(F32)
    mag = jnp.exp(lam_re * dt)
    ab_re = mag * jnp.cos(lam_im * dt)
    ab_im = mag * jnp.sin(lam_im * dt)
    den = lam_re * lam_re + lam_im * lam_im
    n_re = ab_re - 1.0
    n_im = ab_im
    k_re = (n_re * lam_re + n_im * lam_im) / den
    k_im = (n_im * lam_re - n_re * lam_im) / den
    b_re = b_re.astype(F32)
    b_im = b_im.astype(F32)
    bb_re = k_re[..., None] * b_re - k_im[..., None] * b_im
    bb_im = k_re[..., None] * b_im + k_im[..., None] * b_re
    magt = jnp.exp(lam_re * dt * tsub)
    at_re = magt * jnp.cos(lam_im * dt * tsub)
    at_im = magt * jnp.sin(lam_im * dt * tsub)
    eye = jnp.eye(S5_BUNDLE_GROUPS, dtype=F32)

    def pack_b(bb):
        bb = bb.reshape(nb, S5_BUNDLE_GROUPS, S5_STATE, S5_GROUP_CH)
        return jnp.einsum("jgph,gk->jghkp", bb, eye).reshape(nb, S5_BUNDLE_CH, S5_BUNDLE_STATE)

    def pack_c(c):
        c = c.astype(F32).reshape(nb, S5_BUNDLE_GROUPS, S5_GROUP_CH, S5_STATE)
        return jnp.einsum("jghp,gk->jkpgh", c, eye).reshape(nb, S5_BUNDLE_STATE, S5_BUNDLE_CH)

    def pack_a(re, im):
        return jnp.concatenate([re.reshape(nb, 1, S5_BUNDLE_STATE), im.reshape(nb, 1, S5_BUNDLE_STATE)], axis=-1)

    bmat = jnp.concatenate([pack_b(bb_re), pack_b(bb_im)], axis=-1).astype(BF16)
    cmat = jnp.concatenate([pack_c(c_re), -pack_c(c_im)], axis=1).astype(BF16)
    return bmat, cmat, pack_a(ab_re, ab_im), pack_a(at_re, at_im)


def s5_mixer(xin, lam_re, lam_im, log_dt, b_re, b_im, c_re, c_im, d, w_glu, b_glu, *, tc=512):
    bsz, seq, width = xin.shape
    tc = min(tc, seq)
    tsub = tc // S5_SEGMENTS
    nb = width // S5_BUNDLE_CH
    nchunk = seq // tc
    bmat, cmat, a, at = _s5_tables(lam_re, lam_im, log_dt, b_re, b_im, c_re, c_im, tsub)
    xp = xin.reshape(bsz, nchunk, S5_SEGMENTS, tsub, nb, S5_BUNDLE_CH)
    xp = jnp.transpose(xp, (0, 4, 1, 3, 2, 5)).reshape(bsz, nb, seq, S5_BUNDLE_CH)
    ns2 = 2 * S5_BUNDLE_STATE
    out = pl.pallas_call(
        functools.partial(_s5_kernel, tsub=tsub),
        grid=(bsz, nchunk),
        in_specs=[pl.BlockSpec((None, nb, tc, S5_BUNDLE_CH), lambda b, c: (b, 0, c, 0)),
                  pl.BlockSpec((nb, S5_BUNDLE_CH, ns2), lambda b, c: (0, 0, 0)),
                  pl.BlockSpec((nb, ns2, S5_BUNDLE_CH), lambda b, c: (0, 0, 0)),
                  pl.BlockSpec((nb, 1, ns2), lambda b, c: (0, 0, 0)),
                  pl.BlockSpec((nb, 1, ns2), lambda b, c: (0, 0, 0)),
                  pl.BlockSpec((nb, 1, S5_BUNDLE_CH), lambda b, c: (0, 0, 0)),
                  pl.BlockSpec((nb, S5_BUNDLE_CH, width), lambda b, c: (0, 0, 0)),
                  pl.BlockSpec((1, width), lambda b, c: (0, 0))],
        out_specs=pl.BlockSpec((None, tc, width), lambda b, c: (b, c, 0)),
        out_shape=jax.ShapeDtypeStruct((bsz, seq, width), BF16),
        scratch_shapes=[pltpu.VMEM((2, tc, ns2), F32),
                        pltpu.VMEM((nb, tc, S5_BUNDLE_CH), F32),
                        pltpu.VMEM((nb, S5_SEGMENTS, ns2), F32),
                        pltpu.VMEM((nb, S5_SEGMENTS, ns2), F32),
                        pltpu.VMEM((nb, 1, ns2), F32)],
        compiler_params=_params("parallel", "arbitrary"),
        name="s5_mixer",
    )(xp, bmat, cmat, a, at, d.reshape(nb, 1, S5_BUNDLE_CH).astype(F32),
      w_glu.astype(BF16).reshape(nb, S5_BUNDLE_CH, width), b_glu.reshape(1, width).astype(F32))
    out = out.reshape(bsz, nchunk, tsub, S5_SEGMENTS, width)
    return jnp.transpose(out, (0, 1, 3, 2, 4)).reshape(bsz, seq, width)


def _merge_kernel(ya_ref, yb_ref, yc_ref, ga_ref, gb_ref, gc_ref, w_ref, o_ref):
    acc = ga_ref[...].astype(F32) * jnp.dot(ya_ref[...], w_ref[0], preferred_element_type=F32)
    acc += gb_ref[...].astype(F32) * jnp.dot(yb_ref[...], w_ref[1], preferred_element_type=F32)
    acc += gc_ref[...].astype(F32) * jnp.dot(yc_ref[...], w_ref[2], preferred_element_type=F32)
    o_ref[...] = acc.astype(o_ref.dtype)


def merge_branches(ya, yb, yc, gates, gate_col, w_branch, *, tm=512):
    n, width = ya.shape
    d = w_branch.shape[2]
    assert gate_col % d == 0
    y_spec = pl.BlockSpec((tm, width), lambda i: (i, 0))

    def gate_spec(branch):
        return pl.BlockSpec((tm, d), lambda i: (i, gate_col // d + branch))

    return pl.pallas_call(
        _merge_kernel,
        grid=(n // tm,),
        in_specs=[y_spec, y_spec, y_spec, gate_spec(0), gate_spec(1), gate_spec(2),
                  pl.BlockSpec((3, width, d), lambda i: (0, 0, 0))],
        out_specs=pl.BlockSpec((tm, d), lambda i: (i, 0)),
        out_shape=jax.ShapeDtypeStruct((n, d), BF16),
        compiler_params=_params("parallel"),
        name="merge_branches",
    )(ya, yb, yc, gates, gates, gates, w_branch)


def kernel(x, norm1_g, w_in, b_gate, gm_norm_g, gm_w_s, gm_b_s, s5_lambda_re, s5_lambda_im, s5_log_dt,
           s5_b_re, s5_b_im, s5_c_re, s5_c_im, s5_d, s5_w_glu, s5_b_glu, w_branch, w_out, norm2_g,
           w_mlp_in, w_mlp_out, final_g):
    bsz, seq, d_model = x.shape
    depth = w_in.shape[0]
    width = gm_norm_g.shape[1]
    heads = width // SB_HEAD_DIM
    o_a = 2 * width
    o_b = o_a + width
    o_c = o_b + 3 * width
    n = bsz * seq
    xf = x.reshape(n, d_model)
    col = jnp.arange(w_in.shape[2])
    col_scale = jnp.where((col >= o_b) & (col < o_b + width), SB_LOGIT_SCALE, 1.0).astype(F32)
    for l in range(depth):
        gate_bias = jnp.concatenate([jnp.zeros((o_c,), F32), b_gate[l].reshape(-1).astype(F32)])
        proj = matmul(xf, w_in, l, norm_g=norm1_g[l], scale=col_scale, bias=gate_bias, act="sigmoid",
                      act_from_col=o_c, tn=1024)
        y_a = gmlp(proj, gm_norm_g[l], gm_w_s[l], gm_b_s[l], width=width)
        proj3 = proj.reshape(bsz, seq, proj.shape[1])
        y_b = s5_mixer(proj3[:, :, o_a:o_b], s5_lambda_re[l], s5_lambda_im[l], s5_log_dt[l],
                       s5_b_re[l], s5_b_im[l], s5_c_re[l], s5_c_im[l], s5_d[l], s5_w_glu[l], s5_b_glu[l])
        y_c = sb_attention(proj3, o_b, o_b + width, o_b + 2 * width, heads=heads)
        merged = merge_branches(y_a, y_b.reshape(n, width), y_c.reshape(n, width), proj, o_c,
                                w_branch[l].astype(BF16))
        xf = matmul(merged, w_out, l, residual=xf, out_dtype=F32)
        act = matmul(xf, w_mlp_in, l, norm_g=norm2_g[l], act="relu2", tn=1024)
        xf = matmul(act, w_mlp_out, l, residual=xf, out_dtype=F32, tn=1024, tk=1024)
    return rmsnorm(xf, final_g, x.dtype).reshape(bsz, seq, d_model)
```

```python
import functools
import math

import jax
import jax.numpy as jnp
from jax import lax
from jax.experimental import pallas as pl
from jax.experimental.pallas import tpu as pltpu

F32 = jnp.float32
BF16 = jnp.bfloat16

EPS = 1e-6
LANES = 128
SUBLANES = 8
MXU_WIDTH = 256
VMEM_LIMIT_BYTES = 60 * 1024 * 1024

GM_CHUNK = 128
GM_GROUP_CH = 128
S5_GROUP_CH = 16
S5_STATE = 64
S5_BUNDLE_GROUPS = 8
S5_BUNDLE_CH = S5_BUNDLE_GROUPS * S5_GROUP_CH
S5_BUNDLE_STATE = S5_BUNDLE_GROUPS * S5_STATE
S5_SEGMENTS = SUBLANES
SB_HEAD_DIM = 128


def _params(*semantics):
    return pltpu.CompilerParams(dimension_semantics=semantics, vmem_limit_bytes=VMEM_LIMIT_BYTES)


def _rmsnorm_kernel(x_ref, g_ref, o_ref):
    x = x_ref[...]
    y = x * lax.rsqrt(jnp.mean(x * x, axis=-1, keepdims=True) + EPS)
    o_ref[...] = (y * g_ref[...]).astype(o_ref.dtype)


def rmsnorm(x, g, out_dtype, tm=512):
    n, d = x.shape
    tm = min(tm, n)
    return pl.pallas_call(
        _rmsnorm_kernel,
        grid=(n // tm,),
        in_specs=[pl.BlockSpec((tm, d), lambda i: (i, 0)),
                  pl.BlockSpec((1, d), lambda i: (0, 0))],
        out_specs=pl.BlockSpec((tm, d), lambda i: (i, 0)),
        out_shape=jax.ShapeDtypeStruct((n, d), out_dtype),
        compiler_params=_params("parallel"),
        name="rmsnorm",
    )(x, g.reshape(1, d).astype(F32))


def _matmul_kernel(*refs, nk, act, act_from, has_norm, has_affine, has_res):
    refs = list(refs)
    x_ref = refs.pop(0)
    g_ref = refs.pop(0) if has_norm else None
    w_ref = refs.pop(0)
    s_ref, b_ref = (refs.pop(0), refs.pop(0)) if has_affine else (None, None)
    r_ref = refs.pop(0) if has_res else None
    o_ref = refs.pop(0)
    h_ref = refs.pop(0) if has_norm else None
    j = pl.program_id(1)

    if has_norm:
        @pl.when(j == 0)
        def _():
            x = x_ref[...]
            y = x * lax.rsqrt(jnp.mean(x * x, axis=-1, keepdims=True) + EPS)
            h_ref[...] = (y * g_ref[...]).astype(BF16)

        lhs_ref = h_ref
    else:
        lhs_ref = x_ref

    if nk > 1:
        @pl.when(pl.program_id(2) == 0)
        def _():
            o_ref[...] = r_ref[...] if has_res else jnp.zeros_like(o_ref)

    for c in range(o_ref.shape[1] // MXU_WIDTH):
        cols = slice(c * MXU_WIDTH, (c + 1) * MXU_WIDTH)
        acc = jnp.dot(lhs_ref[...], w_ref[:, cols].astype(BF16), preferred_element_type=F32)
        if nk > 1:
            o_ref[:, cols] += acc
            continue
        if has_affine:
            acc = acc * s_ref[:, cols] + b_ref[:, cols]
        if act is not None:
            activated = jax.nn.sigmoid(acc) if act == "sigmoid" else jnp.square(jnp.maximum(acc, 0.0))
            acc = activated if act_from == 0 else jnp.where(j >= act_from, activated, acc)
        if has_res:
            acc = acc + r_ref[:, cols]
        o_ref[:, cols] = acc.astype(o_ref.dtype)


def matmul(x, w, layer, *, norm_g=None, scale=None, bias=None, act=None, act_from_col=0, residual=None,
           out_dtype=BF16, tm=2048, tn=512, tk=2048):
    m, kdim = x.shape
    n = w.shape[2]
    tm, tn, tk = min(tm, m), min(tn, n), min(tk, kdim)
    nk = kdim // tk
    has_norm = norm_g is not None
    has_affine = scale is not None or bias is not None
    assert not has_norm or nk == 1, "the fused rmsnorm needs whole rows"
    assert nk == 1 or (out_dtype == F32 and act is None and not has_affine), "k-split accumulates in the output"
    assert act_from_col % tn == 0
    in_specs = [pl.BlockSpec((tm, tk), lambda i, j, k: (i, k))]
    args = [x]
    if has_norm:
        in_specs.append(pl.BlockSpec((1, kdim), lambda i, j, k: (0, 0)))
        args.append(norm_g.reshape(1, kdim).astype(F32))
    in_specs.append(pl.BlockSpec((None, tk, tn), lambda i, j, k: (layer, k, j)))
    args.append(w)
    if has_affine:
        row_spec = pl.BlockSpec((1, tn), lambda i, j, k: (0, j))
        in_specs += [row_spec, row_spec]
        args.append((jnp.ones((n,), F32) if scale is None else scale.astype(F32)).reshape(1, n))
        args.append((jnp.zeros((n,), F32) if bias is None else bias.astype(F32)).reshape(1, n))
    if residual is not None:
        in_specs.append(pl.BlockSpec((tm, tn), lambda i, j, k: (i, j)))
        args.append(residual)
    scratch = []
    if has_norm:
        scratch.append(pltpu.VMEM((tm, kdim), BF16))
    kern =functools.partial(_matmul_kernel, nk=nk, act=act, act_from=act_from_col // tn, has_norm=has_norm,
                             has_affine=has_affine, has_res=residual is not None)
    return pl.pallas_call(
        kern,
        grid=(m // tm, n // tn, nk),
        in_specs=in_specs,
        out_specs=pl.BlockSpec((tm, tn), lambda i, j, k: (i, j)),
        out_shape=jax.ShapeDtypeStruct((m, n), out_dtype),
        scratch_shapes=scratch,
        compiler_params=_params("parallel", "arbitrary", "arbitrary"),
        name="matmul",
    )(*args)


def _gmlp_kernel(uv_ref, g_ref, w_ref, bt_ref, o_ref, *, width):
    z = jax.nn.gelu(uv_ref[...].astype(F32))
    u = z[:, :width]
    v = z[:, width:]
    v = v * lax.rsqrt(jnp.mean(v * v, axis=-1, keepdims=True) + EPS) * g_ref[...]
    v = v.astype(BF16)
    tm = u.shape[0]
    t_idx = lax.broadcasted_iota(jnp.int32, (GM_CHUNK, GM_CHUNK), 0)
    s_idx = lax.broadcasted_iota(jnp.int32, (GM_CHUNK, GM_CHUNK), 1)
    causal = s_idx <= t_idx
    for g in range(width // GM_GROUP_CH):
        w = jnp.where(causal, w_ref[g], 0.0).astype(BF16)
        bias = bt_ref[:, g:g + 1]
        cols = slice(g * GM_GROUP_CH, (g + 1) * GM_GROUP_CH)
        for c in range(tm // GM_CHUNK):
            rows = slice(c * GM_CHUNK, (c + 1) * GM_CHUNK)
            mixed = jnp.dot(w, v[rows, cols], preferred_element_type=F32) + bias
            o_ref[rows, cols] = (u[rows, cols] * mixed).astype(o_ref.dtype)


def gmlp(proj, norm_g, w_s, b_s, *, width, tm=256):
    n = proj.shape[0]
    groups = width // GM_GROUP_CH
    return pl.pallas_call(
        functools.partial(_gmlp_kernel, width=width),
        grid=(n // tm,),
        in_specs=[pl.BlockSpec((tm, 2 * width), lambda i: (i, 0)),
                  pl.BlockSpec((1, width), lambda i: (0, 0)),
                  pl.BlockSpec((groups, GM_CHUNK, GM_CHUNK), lambda i: (0, 0, 0)),
                  pl.BlockSpec((GM_CHUNK, groups), lambda i: (0, 0))],
        out_specs=pl.BlockSpec((tm, width), lambda i: (i, 0)),
        out_shape=jax.ShapeDtypeStruct((n, width), BF16),
        compiler_params=_params("parallel"),
        name="gmlp",
    )(proj, norm_g.reshape(1, width).astype(F32), w_s.astype(F32), jnp.transpose(b_s).astype(F32))


SB_LOGIT_SCALE = SB_HEAD_DIM ** -0.5 * math.log2(math.e)


SB_MASKED_LOGIT = -1e30


def _sb_attn_kernel(q_ref, k_ref, v_ref, tri_ref, o_ref, z_scr, p_scr, tot_scr, *, tq, blk, hps):
    qi = pl.program_id(2)
    dh = SB_HEAD_DIM
    tri = tri_ref[...]
    nblk = tq // blk

    def logits(h, start, c, diagonal):
        hc = slice(h * dh, (h + 1) * dh)
        k = k_ref[pl.ds(pl.multiple_of(start + c * blk, blk), blk), hc]
        z = lax.dot_general(q_ref[:, hc], k, (((1,), (1,)), ((), ())), preferred_element_type=F32)
        if diagonal:
            t_idx = lax.broadcasted_iota(jnp.int32, (tq, blk), 0)
            s_idx = c * blk + lax.broadcasted_iota(jnp.int32, (tq, blk), 1)
            z = jnp.where(s_idx < t_idx, z, SB_MASKED_LOGIT)
        neg_abs = pltpu.bitcast(pltpu.bitcast(z, jnp.uint32) | jnp.uint32(0x80000000), F32)
        p = jnp.maximum(z, 0.0) + jnp.log2(1.0 + jnp.exp2(neg_abs))
        return z - p, p.astype(BF16), jnp.sum(p, axis=1, keepdims=True)

    def stash(h, c, vals):
        z_scr[h, c], p_scr[h, c], tot_scr[h, c] = vals

    def weights_step(h, start, c, run, acc):
        hc = slice(h * dh, (h + 1) * dh)
        cs = jnp.dot(p_scr[h, c], tri, preferred_element_type=F32)
        w = jnp.exp2(z_scr[h, c] - cs - run).astype(BF16)
        v = v_ref[pl.ds(pl.multiple_of(start + c * blk, blk), blk), hc]
        return run + tot_scr[h, c], acc + jnp.dot(w, v, preferred_element_type=F32)

    for h in range(hps):
        for c in reversed(range(nblk)):
            stash(h, c, logits(h, qi * tq, c, True))

    def trip(i, carry):
        cur = (qi - i) * tq
        out = []
        for h in range(hps):
            run, acc = carry[h]
            for c in reversed(range(nblk)):
                run, acc = weights_step(h, cur, c, run, acc)
                stash(h, c, logits(h, cur - tq, c, False))
            out.append((run, acc))
        return tuple(out)

    carry = tuple((jnp.zeros((tq, 1), F32), jnp.zeros((tq, dh), F32)) for _ in range(hps))
    carry = lax.fori_loop(0, qi, trip, carry)
    for h in range(hps):
        run, acc = carry[h]
        for c in reversed(range(nblk)):
            run, acc = weights_step(h, 0, c, run, acc)
        o_ref[:, h * dh:(h + 1) * dh] = acc.astype(o_ref.dtype)


def sb_attention(proj, q_col, k_col, v_col, *, heads, tq=512, blk=256, hps=2):
    bsz, seq, _ = proj.shape
    hw = hps * SB_HEAD_DIM
    tq = min(tq, seq)
    j_idx = jnp.arange(blk)
    tri = (j_idx[:, None] > j_idx[None, :]).astype(BF16)
    kern = functools.partial(_sb_attn_kernel, tq=tq, blk=blk, hps=hps)
    return pl.pallas_call(
        kern,
        grid=(bsz, heads // hps, seq // tq),
        in_specs=[pl.BlockSpec((None, tq, hw), lambda b, h, i: (b, i, q_col // hw + h)),
                  pl.BlockSpec((None, seq, hw), lambda b, h, i: (b, 0, k_col // hw + h)),
                  pl.BlockSpec((None, seq, hw), lambda b, h, i: (b, 0, v_col // hw + h)),
                  pl.BlockSpec((blk, blk), lambda b, h, i: (0, 0))],
        out_specs=pl.BlockSpec((None, tq, hw), lambda b, h, i: (b, i, h)),
        out_shape=jax.ShapeDtypeStruct((bsz, seq, heads * SB_HEAD_DIM), BF16),
        scratch_shapes=[pltpu.VMEM((hps, tq // blk, tq, blk), F32),
                        pltpu.VMEM((hps, tq // blk, tq, blk), BF16),
                        pltpu.VMEM((hps, tq // blk, tq, 1), F32)],
        compiler_params=_params("parallel", "parallel", "arbitrary"),
        name="sb_attention",
    )(proj, proj, proj, tri)


def _s5_kernel(x_ref, bmat_ref, cmat_ref, a_ref, at_ref, d_ref, wglu_ref, bglu_ref, o_ref,
               bu_ref, g_ref, fin_ref, sin_ref, carry_ref, *, tsub):
    nb = x_ref.shape[0]
    ns = S5_BUNDLE_STATE
    tc = tsub * S5_SEGMENTS

    @pl.when(pl.program_id(1) == 0)
    def _():
        carry_ref[...] = jnp.zeros_like(carry_ref)

    def cmul_add(ar, ai, sr, si, br, bi):
        return ar * sr - ai * si + br, ar * si + ai * sr + bi

    def b_matmul(j):
        bu_ref[j % 2] = jnp.dot(x_ref[j], bmat_ref[j], preferred_element_type=F32)

    def scan(j):
        slot = j % 2
        a = a_ref[j]
        ar = jnp.broadcast_to(a[:, :ns], (S5_SEGMENTS, ns))
        ai = jnp.broadcast_to(a[:, ns:], (S5_SEGMENTS, ns))

        def rows(i):
            return slice(i * S5_SEGMENTS, (i + 1) * S5_SEGMENTS)

        sr = si = jnp.zeros((S5_SEGMENTS, ns), F32)
        for i in range(tsub):
            sr, si = cmul_add(ar, ai, sr, si, bu_ref[slot, rows(i), :ns], bu_ref[slot, rows(i), ns:])
        fin_ref[j, :, :ns] = sr
        fin_ref[j, :, ns:] = si
        at = at_ref[j]
        atr, ati = at[:, :ns], at[:, ns:]
        pr = carry_ref[j, :, :ns]
        pi = carry_ref[j, :, ns:]
        for k in range(S5_SEGMENTS):
            sin_ref[j, k:k + 1, :ns] = pr
            sin_ref[j, k:k + 1, ns:] = pi
            pr, pi = cmul_add(atr, ati, pr, pi, fin_ref[j, k:k + 1, :ns], fin_ref[j, k:k + 1, ns:])
        carry_ref[j, :, :ns] = pr
        carry_ref[j, :, ns:] = pi
        sr, si = sin_ref[j, :, :ns], sin_ref[j, :, ns:]
        for i in range(tsub):
            sr, si = cmul_add(ar, ai, sr, si, bu_ref[slot, rows(i), :ns], bu_ref[slot, rows(i), ns:])
            bu_ref[slot, rows(i), :ns] = sr
            bu_ref[slot, rows(i), ns:] = si

    def c_matmul(j):
        y = jnp.dot(bu_ref[j % 2].astype(BF16), cmat_ref[j], preferred_element_type=F32)
        y = y + d_ref[j] * x_ref[j].astype(F32)
        g_ref[j] = jax.nn.gelu(y)

    b_matmul(0)
    for j in range(nb):
        if j + 1 < nb:
            b_matmul(j + 1)
        scan(j)
        c_matmul(j)

    gate = jnp.zeros((tc, nb * S5_BUNDLE_CH), F32)
    for j in range(nb):
        gate = gate + jnp.dot(g_ref[j].astype(BF16), wglu_ref[j], preferred_element_type=F32)
    gate = jax.nn.sigmoid(gate + bglu_ref[...])
    for j in range(nb):
        cols = slice(j * S5_BUNDLE_CH, (j + 1) * S5_BUNDLE_CH)
        o_ref[:, cols] = (g_ref[j] * gate[:, cols]).astype(o_ref.dtype)


def _s5_tables(lam_re, lam_im, log_dt, b_re, b_im, c_re, c_im, tsub):
    groups = lam_re.shape[0]
    nb = groups // S5_BUNDLE_GROUPS
    dt = jnp.exp(log_dt.astype(F32))[:, None]
    lam_re = lam_re.astype(F32)
    lam_im = lam_im.astype(F32)
    mag = jnp.exp(lam_re * dt)
    ab_re = mag * jnp.cos(lam_im * dt)
    ab_im = mag * jnp.sin(lam_im * dt)
    den = lam_re * lam_re + lam_im * lam_im
    n_re = ab_re - 1.0
    n_im = ab_im
    k_re = (n_re * lam_re + n_im * lam_im) / den
    k_im = (n_im * lam_re - n_re * lam_im) / den
    b_re = b_re.astype(F32)
    b_im = b_im.astype(F32)
    bb_re = k_re[..., None] * b_re - k_im[..., None] * b_im
    bb_im = k_re[..., None] * b_im + k_im[..., None] * b_re
    magt = jnp.exp(lam_re * dt * tsub)
    at_re = magt * jnp.cos(lam_im * dt * tsub)
    at_im = magt * jnp.sin(lam_im * dt * tsub)
    eye = jnp.eye(S5_BUNDLE_GROUPS, dtype=F32)

    def pack_b(bb):
        bb = bb.reshape(nb, S5_BUNDLE_GROUPS, S5_STATE, S5_GROUP_CH)
        return jnp.einsum("jgph,gk->jghkp", bb, eye).reshape(nb, S5_BUNDLE_CH, S5_BUNDLE_STATE)

    def pack_c(c):
        c = c.astype(F32).reshape(nb, S5_BUNDLE_GROUPS, S5_GROUP_CH, S5_STATE)
        return jnp.einsum("jghp,gk->jkpgh", c, eye).reshape(nb, S5_BUNDLE_STATE, S5_BUNDLE_CH)

    def pack_a(re, im):
        return jnp.concatenate([re.reshape(nb, 1, S5_BUNDLE_STATE), im.reshape(nb, 1, S5_BUNDLE_STATE)], axis=-1)

    bmat = jnp.concatenate([pack_b(bb_re), pack_b(bb_im)], axis=-1).astype(BF16)
    cmat = jnp.concatenate([pack_c(c_re), -pack_c(c_im)], axis=1).astype(BF16)
    return bmat, cmat, pack_a(ab_re, ab_im), pack_a(at_re, at_im)


def s5_mixer(xin, lam_re, lam_im, log_dt, b_re, b_im, c_re, c_im, d, w_glu, b_glu, *, tc=512):
    bsz, seq, width = xin.shape
    tc = min(tc, seq)
    tsub = tc // S5_SEGMENTS
    nb = width // S5_BUNDLE_CH
    nchunk = seq // tc
    bmat, cmat, a, at = _s5_tables(lam_re, lam_im, log_dt, b_re, b_im, c_re, c_im, tsub)
    xp = xin.reshape(bsz, nchunk, S5_SEGMENTS, tsub, nb, S5_BUNDLE_CH)
    xp = jnp.transpose(xp, (0, 4, 1, 3, 2, 5)).reshape(bsz, nb, seq, S5_BUNDLE_CH)
    ns2 = 2 * S5_BUNDLE_STATE
    out = pl.pallas_call(
        functools.partial(_s5_kernel, tsub=tsub),
        grid=(bsz, nchunk),
        in_specs=[pl.BlockSpec((None, nb, tc, S5_BUNDLE_CH), lambda b, c: (b, 0, c, 0)),
                  pl.BlockSpec((nb, S5_BUNDLE_CH, ns2), lambda b, c: (0, 0, 0)),
                  pl.BlockSpec((nb, ns2, S5_BUNDLE_CH), lambda b, c: (0, 0, 0)),
                  pl.BlockSpec((nb, 1, ns2), lambda b, c: (0, 0, 0)),
                  pl.BlockSpec((nb, 1, ns2), lambda b, c: (0, 0, 0)),
                  pl.BlockSpec((nb, 1, S5_BUNDLE_CH), lambda b, c: (0, 0, 0)),
                  pl.BlockSpec((nb, S5_BUNDLE_CH, width), lambda b, c: (0, 0, 0)),
                  pl.BlockSpec((1, width), lambda b, c: (0, 0))],
        out_specs=pl.BlockSpec((None, tc, width), lambda b, c: (b, c, 0)),
        out_shape=jax.ShapeDtypeStruct((bsz, seq, width), BF16),
        scratch_shapes=[pltpu.VMEM((2, tc, ns2), F32),
                        pltpu.VMEM((nb, tc, S5_BUNDLE_CH), F32),
                        pltpu.VMEM((nb, S5_SEGMENTS, ns2), F32),
                        pltpu.VMEM((nb, S5_SEGMENTS, ns2), F32),
                        pltpu.VMEM((nb, 1, ns2), F32)],
        compiler_params=_params("parallel", "arbitrary"),
        name="s5_mixer",
    )(xp, bmat, cmat, a, at, d.reshape(nb, 1, S5_BUNDLE_CH).astype(F32),
      w_glu.astype(BF16).reshape(nb, S5_BUNDLE_CH, width), b_glu.reshape(1, width).astype(F32))
    out = out.reshape(bsz, nchunk, tsub, S5_SEGMENTS, width)
    return jnp.transpose(out, (0, 1, 3, 2, 4)).reshape(bsz, seq, width)


def _merge_kernel(ya_ref, yb_ref, yc_ref, ga_ref, gb_ref, gc_ref, w_ref, o_ref):
    for c in range(o_ref.shape[1] // MXU_WIDTH):
        cols = slice(c * MXU_WIDTH, (c + 1) * MXU_WIDTH)
        acc = jnp.zeros((o_ref.shape[0], MXU_WIDTH), F32)
        for n, (y_ref, g_ref) in enumerate(((ya_ref, ga_ref), (yb_ref, gb_ref), (yc_ref, gc_ref))):
            gate = jax.nn.sigmoid(g_ref[:, cols].astype(F32))
            acc += gate * jnp.dot(y_ref[...], w_ref[n, :, cols], preferred_element_type=F32)
        o_ref[:, cols] = acc.astype(o_ref.dtype)


def merge_branches(ya, yb, yc, gates, gate_col, w_branch, *, tm=512):
    n, width = ya.shape
    d = w_branch.shape[2]
    assert gate_col % d == 0
    y_spec = pl.BlockSpec((tm, width), lambda i: (i, 0))

    def gate_spec(branch):
        return pl.BlockSpec((tm, d), lambda i: (i, gate_col // d + branch))

    return pl.pallas_call(
        _merge_kernel,
        grid=(n // tm,),
        in_specs=[y_spec, y_spec, y_spec, gate_spec(0), gate_spec(1), gate_spec(2),
                  pl.BlockSpec((3, width, d), lambda i: (0, 0, 0))],
        out_specs=pl.BlockSpec((tm, d), lambda i: (i, 0)),
        out_shape=jax.ShapeDtypeStruct((n, d), BF16),
        compiler_params=_params("parallel"),
        name="merge_branches",
    )(ya, yb, yc, gates, gates, gates, w_branch)


def kernel(x, norm1_g, w_in, b_gate, gm_norm_g, gm_w_s, gm_b_s, s5_lambda_re, s5_lambda_im, s5_log_dt,
           s5_b_re, s5_b_im, s5_c_re, s5_c_im, s5_d, s5_w_glu, s5_b_glu, w_branch, w_out, norm2_g,
           w_mlp_in, w_mlp_out, final_g):
    bsz, seq, d_model = x.shape
    depth = w_in.shape[0]
    width = gm_norm_g.shape[1]
    heads = width // SB_HEAD_DIM
    o_a = 2 * width
    o_b = o_a + width
    o_c = o_b + 3 * width
    n = bsz * seq
    xf = x.reshape(n, d_model)
    col = jnp.arange(w_in.shape[2])
    col_scale = jnp.where((col >= o_b) & (col < o_b + width), SB_LOGIT_SCALE, 1.0).astype(F32)
    for l in range(depth):
        gate_bias = jnp.concatenate([jnp.zeros((o_c,), F32), b_gate[l].reshape(-1).astype(F32)])
        proj = matmul(xf, w_in, l, norm_g=norm1_g[l], scale=col_scale, bias=gate_bias)
        y_a = gmlp(proj, gm_norm_g[l], gm_w_s[l], gm_b_s[l], width=width)
        proj3 = proj.reshape(bsz, seq, proj.shape[1])
        y_b = s5_mixer(proj3[:, :, o_a:o_b], s5_lambda_re[l], s5_lambda_im[l], s5_log_dt[l],
                       s5_b_re[l], s5_b_im[l], s5_c_re[l], s5_c_im[l], s5_d[l], s5_w_glu[l], s5_b_glu[l])
        y_c = sb_attention(proj3, o_b, o_b + width, o_b + 2 * width, heads=heads)
        merged = merge_branches(y_a, y_b.reshape(n, width), y_c.reshape(n, width), proj, o_c,
                                w_branch[l].astype(BF16))
        xf = matmul(merged, w_out, l, residual=xf, out_dtype=F32)
        act = matmul(xf, w_mlp_in, l, norm_g=norm2_g[l], act="relu2")
        xf = matmul(act, w_mlp_out, l, residual=xf, out_dtype=F32, tn=1024, tk=1024)
    return rmsnorm(xf, final_g, x.dtype).reshape(bsz, seq, d_model)
```

```python
import functools
import math

import jax
import jax.numpy as jnp
from jax import lax
from jax.experimental import pallas as pl
from jax.experimental.pallas import tpu as pltpu

F32 = jnp.float32
BF16 = jnp.bfloat16

EPS = 1e-6
LANES = 128
SUBLANES = 8
MXU_WIDTH = 256
VMEM_LIMIT_BYTES = 60 * 1024 * 1024

GM_CHUNK = 128
GM_GROUP_CH = 128
S5_GROUP_CH = 16
S5_STATE = 64
S5_BUNDLE_GROUPS = 8
S5_BUNDLE_CH = S5_BUNDLE_GROUPS * S5_GROUP_CH
S5_BUNDLE_STATE = S5_BUNDLE_GROUPS * S5_STATE
S5_SEGMENTS = SUBLANES
SB_HEAD_DIM = 128


def _params(*semantics):
    return pltpu.CompilerParams(dimension_semantics=semantics, vmem_limit_bytes=VMEM_LIMIT_BYTES)


def _rmsnorm_kernel(x_ref, g_ref, o_ref):
    x = x_ref[...]
    y = x * lax.rsqrt(jnp.mean(x * x, axis=-1, keepdims=True) + EPS)
    o_ref[...] = (y * g_ref[...]).astype(o_ref.dtype)


def rmsnorm(x, g, out_dtype, tm=512):
    n, d = x.shape
    tm = min(tm, n)
    return pl.pallas_call(
        _rmsnorm_kernel,
        grid=(n // tm,),
        in_specs=[pl.BlockSpec((tm, d), lambda i: (i, 0)),
                  pl.BlockSpec((1, d), lambda i: (0, 0))],
        out_specs=pl.BlockSpec((tm, d), lambda i: (i, 0)),
        out_shape=jax.ShapeDtypeStruct((n, d), out_dtype),
        compiler_params=_params("parallel"),
        name="rmsnorm",
    )(x, g.reshape(1, d).astype(F32))


def _matmul_kernel(*refs, nk, act, act_from, has_norm, has_affine, has_res):
    refs = list(refs)
    x_ref = refs.pop(0)
    g_ref = refs.pop(0) if has_norm else None
    w_ref = refs.pop(0)
    s_ref, b_ref = (refs.pop(0), refs.pop(0)) if has_affine else (None, None)
    r_ref = refs.pop(0) if has_res else None
    o_ref = refs.pop(0)
    h_ref = refs.pop(0) if has_norm else None
    j = pl.program_id(1)

    if has_norm:
        @pl.when(j == 0)
        def _():
            x = x_ref[...]
            y = x * lax.rsqrt(jnp.mean(x * x, axis=-1, keepdims=True) + EPS)
            h_ref[...] = (y * g_ref[...]).astype(BF16)

        lhs_ref = h_ref
    else:
        lhs_ref = x_ref

    if nk > 1:
        @pl.when(pl.program_id(2) == 0)
        def _():
            o_ref[...] = r_ref[...] if has_res else jnp.zeros_like(o_ref)

    for c in range(o_ref.shape[1] // MXU_WIDTH):
        cols = slice(c * MXU_WIDTH, (c + 1) * MXU_WIDTH)
        acc = jnp.dot(lhs_ref[...], w_ref[:, cols].astype(BF16), preferred_element_type=F32)
        if nk > 1:
            o_ref[:, cols] += acc
            continue
        if has_affine:
            acc = acc * s_ref[:, cols] + b_ref[:, cols]
        if act is not None:
            activated = jax.nn.sigmoid(acc) if act == "sigmoid" else jnp.square(jnp.maximum(acc, 0.0))
            acc = activated if act_from == 0 else jnp.where(j >= act_from, activated, acc)
        if has_res:
            acc = acc + r_ref[:, cols]
        o_ref[:, cols] = acc.astype(o_ref.dtype)


def matmul(x, w, layer, *, norm_g=None, scale=None, bias=None, act=None, act_from_col=0, residual=None,
           out_dtype=BF16, tm=2048, tn=512, tk=2048):
    m, kdim = x.shape
    n = w.shape[2]
    tm, tn, tk = min(tm, m), min(tn, n), min(tk, kdim)
    nk = kdim // tk
    has_norm = norm_g is not None
    has_affine = scale is not None or bias is not None
    assert not has_norm or nk == 1, "the fused rmsnorm needs whole rows"
    assert nk == 1 or (out_dtype == F32 and act is None and not has_affine), "k-split accumulates in the output"
    assert act_from_col % tn == 0
    in_specs = [pl.BlockSpec((tm, tk), lambda i, j, k: (i, k))]
    args = [x]
    if has_norm:
        in_specs.append(pl.BlockSpec((1, kdim), lambda i, j, k: (0, 0)))
        args.append(norm_g.reshape(1, kdim).astype(F32))
    in_specs.append(pl.BlockSpec((None, tk, tn), lambda i, j, k: (layer, k, j)))
    args.append(w)
    if has_affine:
        row_spec = pl.BlockSpec((1, tn), lambda i, j, k: (0, j))
        in_specs += [row_spec, row_spec]
        args.append((jnp.ones((n,), F32) if scale is None else scale.astype(F32)).reshape(1, n))
        args.append((jnp.zeros((n,), F32) if bias is None else bias.astype(F32)).reshape(1, n))
    if residual is not None:
        in_specs.append(pl.BlockSpec((tm, tn), lambda i, j, k: (i, j)))
        args.append(residual)
    scratch = []
    if has_norm:
        scratch.append(pltpu.VMEM((tm, kdim), BF16))
    kern =functools.partial(_matmul_kernel, nk=nk, act=act, act_from=act_from_col // tn, has_norm=has_norm,
                             has_affine=has_affine, has_res=residual is not None)
    return pl.pallas_call(
        kern,
        grid=(m // tm, n // tn, nk),
        in_specs=in_specs,
        out_specs=pl.BlockSpec((tm, tn), lambda i, j, k: (i, j)),
        out_shape=jax.ShapeDtypeStruct((m, n), out_dtype),
        scratch_shapes=scratch,
        compiler_params=_params("parallel", "arbitrary", "arbitrary"),
        name="matmul",
    )(*args)


def _gmlp_kernel(uv_ref, g_ref, w_ref, bt_ref, o_ref, *, width):
    z = jax.nn.gelu(uv_ref[...].astype(F32))
    u = z[:, :width]
    v = z[:, width:]
    v = v * lax.rsqrt(jnp.mean(v * v, axis=-1, keepdims=True) + EPS) * g_ref[...]
    v = v.astype(BF16)
    tm = u.shape[0]
    t_idx = lax.broadcasted_iota(jnp.int32, (GM_CHUNK, GM_CHUNK), 0)
    s_idx = lax.broadcasted_iota(jnp.int32, (GM_CHUNK, GM_CHUNK), 1)
    causal = s_idx <= t_idx
    for g in range(width // GM_GROUP_CH):
        w = jnp.where(causal, w_ref[g], 0.0).astype(BF16)
        bias = bt_ref[:, g:g + 1]
        cols = slice(g * GM_GROUP_CH, (g + 1) * GM_GROUP_CH)
        for c in range(tm // GM_CHUNK):
            rows = slice(c * GM_CHUNK, (c + 1) * GM_CHUNK)
            mixed = jnp.dot(w, v[rows, cols], preferred_element_type=F32) + bias
            o_ref[rows, cols] = (u[rows, cols] * mixed).astype(o_ref.dtype)


def gmlp(proj, norm_g, w_s, b_s, *, width, tm=256):
    n = proj.shape[0]
    groups = width // GM_GROUP_CH
    return pl.pallas_call(
        functools.partial(_gmlp_kernel, width=width),
        grid=(n // tm,),
        in_specs=[pl.BlockSpec((tm, 2 * width), lambda i: (i, 0)),
                  pl.BlockSpec((1, width), lambda i: (0, 0)),
                  pl.BlockSpec((groups, GM_CHUNK, GM_CHUNK), lambda i: (0, 0, 0)),
                  pl.BlockSpec((GM_CHUNK, groups), lambda i: (0, 0))],
        out_specs=pl.BlockSpec((tm, width), lambda i: (i, 0)),
        out_shape=jax.ShapeDtypeStruct((n, width), BF16),
        compiler_params=_params("parallel"),
        name="gmlp",
    )(proj, norm_g.reshape(1, width).astype(F32), w_s.astype(F32), jnp.transpose(b_s).astype(F32))


SB_LOGIT_SCALE = SB_HEAD_DIM ** -0.5 * math.log2(math.e)


SB_MASKED_LOGIT = -1e30


def _sb_attn_kernel(q_ref, k_ref, v_ref, tri_ref, o_ref, z_scr, p_scr, tot_scr, *, tq, blk, hps):
    qi = pl.program_id(2)
    dh = SB_HEAD_DIM
    tri = tri_ref[...]
    nblk = tq // blk

    def logits(h, start, c, diagonal):
        hc = slice(h * dh, (h + 1) * dh)
        k = k_ref[pl.ds(pl.multiple_of(start + c * blk, blk), blk), hc]
        z = lax.dot_general(q_ref[:, hc], k, (((1,), (1,)), ((), ())), preferred_element_type=F32)
        if diagonal:
            t_idx = lax.broadcasted_iota(jnp.int32, (tq, blk), 0)
            s_idx = c * blk + lax.broadcasted_iota(jnp.int32, (tq, blk), 1)
            z = jnp.where(s_idx < t_idx, z, SB_MASKED_LOGIT)
        neg_abs = pltpu.bitcast(pltpu.bitcast(z, jnp.uint32) | jnp.uint32(0x80000000), F32)
        p = jnp.maximum(z, 0.0) + jnp.log2(1.0 + jnp.exp2(neg_abs))
        return z - p, p.astype(BF16), jnp.sum(p, axis=1, keepdims=True)

    def stash(h, c, vals):
        z_scr[h, c], p_scr[h, c], tot_scr[h, c] = vals

    def weights_step(h, start, c, run, acc):
        hc = slice(h * dh, (h + 1) * dh)
        cs = jnp.dot(p_scr[h, c], tri, preferred_element_type=F32)
        w = jnp.exp2(z_scr[h, c] - cs - run).astype(BF16)
        v = v_ref[pl.ds(pl.multiple_of(start + c * blk, blk), blk), hc]
        return run + tot_scr[h, c], acc + jnp.dot(w, v, preferred_element_type=F32)

    for h in range(hps):
        for c in reversed(range(nblk)):
            stash(h, c, logits(h, qi * tq, c, True))

    def trip(i, carry):
        cur = (qi - i) * tq
        out = []
        for h in range(hps):
            run, acc = carry[h]
            for c in reversed(range(nblk)):
                run, acc = weights_step(h, cur, c, run, acc)
                stash(h, c, logits(h, cur - tq, c, False))
            out.append((run, acc))
        return tuple(out)

    carry = tuple((jnp.zeros((tq, 1), F32), jnp.zeros((tq, dh), F32)) for _ in range(hps))
    carry = lax.fori_loop(0, qi, trip, carry)
    for h in range(hps):
        run, acc = carry[h]
        for c in reversed(range(nblk)):
            run, acc = weights_step(h, 0, c, run, acc)
        o_ref[:, h * dh:(h + 1) * dh] = acc.astype(o_ref.dtype)


def sb_attention(proj, q_col, k_col, v_col, *, heads, tq=512, blk=256, hps=2):
    bsz, seq, _ = proj.shape
    hw = hps * SB_HEAD_DIM
    tq = min(tq, seq)
    j_idx = jnp.arange(blk)
    tri = (j_idx[:, None] > j_idx[None, :]).astype(BF16)
    kern = functools.partial(_sb_attn_kernel, tq=tq, blk=blk, hps=hps)
    return pl.pallas_call(
        kern,
        grid=(bsz, heads // hps, seq // tq),
        in_specs=[pl.BlockSpec((None, tq, hw), lambda b, h, i: (b, i, q_col // hw + h)),
                  pl.BlockSpec((None, seq, hw), lambda b, h, i: (b, 0, k_col // hw + h)),
                  pl.BlockSpec((None, seq, hw), lambda b, h, i: (b, 0, v_col // hw + h)),
                  pl.BlockSpec((blk, blk), lambda b, h, i: (0, 0))],
        out_specs=pl.BlockSpec((None, tq, hw), lambda b, h, i: (b, i, h)),
        out_shape=jax.ShapeDtypeStruct((bsz, seq, heads * SB_HEAD_DIM), BF16),
        scratch_shapes=[pltpu.VMEM((hps, tq // blk, tq, blk), F32),
                        pltpu.VMEM((hps, tq // blk, tq, blk), BF16),
                        pltpu.VMEM((hps, tq // blk, tq, 1), F32)],
        compiler_params=_params("parallel", "parallel", "arbitrary"),
        name="sb_attention",
    )(proj, proj, proj, tri)


def _s5_kernel(x_ref, bmat_ref, cmat_ref, a_ref, at_ref, d_ref, wglu_ref, bglu_ref, o_ref,
               bu_ref, g_ref, fin_ref, sin_ref, carry_ref, *, tsub):
    nb = x_ref.shape[0]
    ns = S5_BUNDLE_STATE
    tc = tsub * S5_SEGMENTS

    @pl.when(pl.program_id(1) == 0)
    def _():
        carry_ref[...] = jnp.zeros_like(carry_ref)

    def cmul_add(ar, ai, sr, si, br, bi):
        return ar * sr - ai * si + br, ar * si + ai * sr + bi

    def b_matmul(j):
        bu_ref[j % 2] = jnp.dot(x_ref[j], bmat_ref[j], preferred_element_type=F32)

    def scan(j):
        slot = j % 2
        a = a_ref[j]
        ar = jnp.broadcast_to(a[:, :ns], (S5_SEGMENTS, ns))
        ai = jnp.broadcast_to(a[:, ns:], (S5_SEGMENTS, ns))

        def rows(i):
            return slice(i * S5_SEGMENTS, (i + 1) * S5_SEGMENTS)

        sr = si = jnp.zeros((S5_SEGMENTS, ns), F32)
        for i in range(tsub):
            sr, si = cmul_add(ar, ai, sr, si, bu_ref[slot, rows(i), :ns], bu_ref[slot, rows(i), ns:])
        fin_ref[j, :, :ns] = sr
        fin_ref[j, :, ns:] = si
        at = at_ref[j]
        atr, ati = at[:, :ns], at[:, ns:]
        pr = carry_ref[j, :, :ns]
        pi = carry_ref[j, :, ns:]
        for k in range(S5_SEGMENTS):
            sin_ref[j, k:k + 1, :ns] = pr
            sin_ref[j, k:k + 1, ns:] = pi
            pr, pi = cmul_add(atr, ati, pr, pi, fin_ref[j, k:k + 1, :ns], fin_ref[j, k:k + 1, ns:])
        carry_ref[j, :, :ns] = pr
        carry_ref[j, :, ns:] = pi
        sr, si = sin_ref[j, :, :ns], sin_ref[j, :, ns:]
        for i in range(tsub):
            sr, si = cmul_add(ar, ai, sr, si, bu_ref[slot, rows(i), :ns], bu_ref[slot, rows(i), ns:])
            bu_ref[slot, rows(i), :ns] = sr
            bu_ref[slot, rows(i), ns:] = si

    def c_matmul(j):
        y = jnp.dot(bu_ref[j % 2].astype(BF16), cmat_ref[j], preferred_element_type=F32)
        y = y + d_ref[j] * x_ref[j].astype(F32)
        g_ref[j] = jax.nn.gelu(y)

    b_matmul(0)
    for j in range(nb):
        if j + 1 < nb:
            b_matmul(j + 1)
        scan(j)
        c_matmul(j)

    gate = jnp.zeros((tc, nb * S5_BUNDLE_CH), F32)
    for j in range(nb):
        gate = gate + jnp.dot(g_ref[j].astype(BF16), wglu_ref[j], preferred_element_type=F32)
    gate = jax.nn.sigmoid(gate + bglu_ref[...])
    for j in range(nb):
        cols = slice(j * S5_BUNDLE_CH, (j + 1) * S5_BUNDLE_CH)
        o_ref[:, cols] = (g_ref[j] * gate[:, cols]).astype(o_ref.dtype)


def _s5_tables(lam_re, lam_im, log_dt, b_re, b_im, c_re, c_im, tsub):
    groups = lam_re.shape[0]
    nb = groups // S5_BUNDLE_GROUPS
    dt = jnp.exp(log_dt.astype(F32))[:, None]
    lam_re = lam_re.astype(F32)
    lam_im = lam_im.astype(F32)
    mag = jnp.exp(lam_re * dt)
    ab_re = mag * jnp.cos(lam_im * dt)
    ab_im = mag * jnp.sin(lam_im * dt)
    den = lam_re * lam_re + lam_im * lam_im
    n_re = ab_re - 1.0
    n_im = ab_im
    k_re = (n_re * lam_re + n_im * lam_im) / den
    k_im = (n_im * lam_re - n_re * lam_im) / den
    b_re = b_re.astype(F32)
    b_im = b_im.astype(F32)
    bb_re = k_re[..., None] * b_re - k_im[..., None] * b_im
    bb_im = k_re[..., None] * b_im + k_im[..., None] * b_re
    magt = jnp.exp(lam_re * dt * tsub)
    at_re = magt * jnp.cos(lam_im * dt * tsub)
    at_im = magt * jnp.sin(lam_im * dt * tsub)
    eye = jnp.eye(S5_BUNDLE_GROUPS, dtype=F32)

    def pack_b(bb):
        bb = bb.reshape(nb, S5_BUNDLE_GROUPS, S5_STATE, S5_GROUP_CH)
        return jnp.einsum("jgph,gk->jghkp", bb, eye).reshape(nb, S5_BUNDLE_CH, S5_BUNDLE_STATE)

    def pack_c(c):
        c = c.astype(F32).reshape(nb, S5_BUNDLE_GROUPS, S5_GROUP_CH, S5_STATE)
        return jnp.einsum("jghp,gk->jkpgh", c, eye).reshape(nb, S5_BUNDLE_STATE, S5_BUNDLE_CH)

    def pack_a(re, im):
        return jnp.concatenate([re.reshape(nb, 1, S5_BUNDLE_STATE), im.reshape(nb, 1, S5_BUNDLE_STATE)], axis=-1)

    bmat = jnp.concatenate([pack_b(bb_re), pack_b(bb_im)], axis=-1).astype(BF16)
    cmat = jnp.concatenate([pack_c(c_re), -pack_c(c_im)], axis=1).astype(BF16)
    return bmat, cmat, pack_a(ab_re, ab_im), pack_a(at_re, at_im)


def s5_mixer(xin, lam_re, lam_im, log_dt, b_re, b_im, c_re, c_im, d, w_glu, b_glu, *, tc=512):
    bsz, seq, width = xin.shape
    tc = min(tc, seq)
    tsub = tc // S5_SEGMENTS
    nb = width // S5_BUNDLE_CH
    nchunk = seq // tc
    bmat, cmat, a, at = _s5_tables(lam_re, lam_im, log_dt, b_re, b_im, c_re, c_im, tsub)
    xp = xin.reshape(bsz, nchunk, S5_SEGMENTS, tsub, nb, S5_BUNDLE_CH)
    xp = jnp.transpose(xp, (0, 4, 1, 3, 2, 5)).reshape(bsz, nb, seq, S5_BUNDLE_CH)
    ns2 = 2 * S5_BUNDLE_STATE
    out = pl.pallas_call(
        functools.partial(_s5_kernel, tsub=tsub),
        grid=(bsz, nchunk),
        in_specs=[pl.BlockSpec((None, nb, tc, S5_BUNDLE_CH), lambda b, c: (b, 0, c, 0)),
                  pl.BlockSpec((nb, S5_BUNDLE_CH, ns2), lambda b, c: (0, 0, 0)),
                  pl.BlockSpec((nb, ns2, S5_BUNDLE_CH), lambda b, c: (0, 0, 0)),
                  pl.BlockSpec((nb, 1, ns2), lambda b, c: (0, 0, 0)),
                  pl.BlockSpec((nb, 1, ns2), lambda b, c: (0, 0, 0)),
                  pl.BlockSpec((nb, 1, S5_BUNDLE_CH), lambda b, c: (0, 0, 0)),
                  pl.BlockSpec((nb, S5_BUNDLE_CH, width), lambda b, c: (0, 0, 0)),
                  pl.BlockSpec((1, width), lambda b, c: (0, 0))],
        out_specs=pl.BlockSpec((None, tc, width), lambda b, c: (b, c, 0)),
        out_shape=jax.ShapeDtypeStruct((bsz, seq, width), BF16),
        scratch_shapes=[pltpu.VMEM((2, tc, ns2), F32),
                        pltpu.VMEM((nb, tc, S5_BUNDLE_CH), F32),
                        pltpu.VMEM((nb, S5_SEGMENTS, ns2), F32),
                        pltpu.VMEM((nb, S5_SEGMENTS, ns2), F32),
                        pltpu.VMEM((nb, 1, ns2), F32)],
        compiler_params=_params("parallel", "arbitrary"),
        name="s5_mixer",
    )(xp, bmat, cmat, a, at, d.reshape(nb, 1, S5_BUNDLE_CH).astype(F32),
      w_glu.astype(BF16).reshape(nb, S5_BUNDLE_CH, width), b_glu.reshape(1, width).astype(F32))
    out = out.reshape(bsz, nchunk, tsub, S5_SEGMENTS, width)
    return jnp.transpose(out, (0, 1, 3, 2, 4)).reshape(bsz, seq, width)


def _merge_kernel(ya_ref, yb_ref, yc_ref, ga_ref, gb_ref, gc_ref, w_ref, o_ref):
    for c in range(o_ref.shape[1] // MXU_WIDTH):
        cols = slice(c * MXU_WIDTH, (c + 1) * MXU_WIDTH)
        acc = jnp.zeros((o_ref.shape[0], MXU_WIDTH), F32)
        for n, (y_ref, g_ref) in enumerate(((ya_ref, ga_ref), (yb_ref, gb_ref), (yc_ref, gc_ref))):
            gate = jax.nn.sigmoid(g_ref[:, cols].astype(F32))
            acc += gate * jnp.dot(y_ref[...], w_ref[n, :, cols].astype(BF16), preferred_element_type=F32)
        o_ref[:, cols] = acc.astype(o_ref.dtype)


def merge_branches(ya, yb, yc, gates, gate_col, w_branch, layer, *, tm=512):
    n, width = ya.shape
    d = w_branch.shape[3]
    assert gate_col % d == 0
    y_spec = pl.BlockSpec((tm, width), lambda i: (i, 0))

    def gate_spec(branch):
        return pl.BlockSpec((tm, d), lambda i: (i, gate_col // d + branch))

    return pl.pallas_call(
        _merge_kernel,
        grid=(n // tm,),
        in_specs=[y_spec, y_spec, y_spec, gate_spec(0), gate_spec(1), gate_spec(2),
                  pl.BlockSpec((None, 3, width, d), lambda i: (layer, 0, 0, 0), pipeline_mode=pl.Buffered(1))],
        out_specs=pl.BlockSpec((tm, d), lambda i: (i, 0)),
        out_shape=jax.ShapeDtypeStruct((n, d), BF16),
        compiler_params=_params("parallel"),
        name="merge_branches",
    )(ya, yb, yc, gates, gates, gates, w_branch)


def kernel(x, norm1_g, w_in, b_gate, gm_norm_g, gm_w_s, gm_b_s, s5_lambda_re, s5_lambda_im, s5_log_dt,
           s5_b_re, s5_b_im, s5_c_re, s5_c_im, s5_d, s5_w_glu, s5_b_glu, w_branch, w_out, norm2_g,
           w_mlp_in, w_mlp_out, final_g):
    bsz, seq, d_model = x.shape
    depth = w_in.shape[0]
    width = gm_norm_g.shape[1]
    heads = width // SB_HEAD_DIM
    o_a = 2 * width
    o_b = o_a + width
    o_c = o_b + 3 * width
    n = bsz * seq
    xf = x.reshape(n, d_model)
    col = jnp.arange(w_in.shape[2])
    col_scale = jnp.where((col >= o_b) & (col < o_b + width), SB_LOGIT_SCALE, 1.0).astype(F32)
    for l in range(depth):
        gate_bias = jnp.concatenate([jnp.zeros((o_c,), F32), b_gate[l].reshape(-1).astype(F32)])
        proj = matmul(xf, w_in, l, norm_g=norm1_g[l], scale=col_scale, bias=gate_bias)
        y_a = gmlp(proj, gm_norm_g[l], gm_w_s[l], gm_b_s[l], width=width)
        proj3 = proj.reshape(bsz, seq, proj.shape[1])
        y_b = s5_mixer(proj3[:, :, o_a:o_b], s5_lambda_re[l], s5_lambda_im[l], s5_log_dt[l],
                       s5_b_re[l], s5_b_im[l], s5_c_re[l], s5_c_im[l], s5_d[l], s5_w_glu[l], s5_b_glu[l])
        y_c = sb_attention(proj3, o_b, o_b + width, o_b + 2 * width, heads=heads)
        merged = merge_branches(y_a, y_b.reshape(n, width), y_c.reshape(n, width), proj, o_c, w_branch, l)
        xf = matmul(merged, w_out, l, residual=xf, out_dtype=F32)
        act = matmul(xf, w_mlp_in, l, norm_g=norm2_g[l], act="relu2")
        xf = matmul(act, w_mlp_out, l, residual=xf, out_dtype=F32, tn=1024, tk=1024)
    return rmsnorm(xf, final_g, x.dtype).reshape(bsz, seq, d_model)
```

```python
import functools
import math

import jax
import jax.numpy as jnp
from jax import lax
from jax.experimental import pallas as pl
from jax.experimental.pallas import tpu as pltpu

F32 = jnp.float32
BF16 = jnp.bfloat16

EPS = 1e-6
LANES = 128
SUBLANES = 8
MXU_WIDTH = 256
VMEM_LIMIT_BYTES = 60 * 1024 * 1024

GM_CHUNK = 128
GM_GROUP_CH = 128
S5_GROUP_CH = 16
S5_STATE = 64
S5_BUNDLE_GROUPS = 8
S5_BUNDLE_CH = S5_BUNDLE_GROUPS * S5_GROUP_CH
S5_BUNDLE_STATE = S5_BUNDLE_GROUPS * S5_STATE
S5_SEGMENTS = SUBLANES
SB_HEAD_DIM = 128


def _params(*semantics):
    return pltpu.CompilerParams(dimension_semantics=semantics, vmem_limit_bytes=VMEM_LIMIT_BYTES)


def _rmsnorm_kernel(x_ref, g_ref, o_ref):
    x = x_ref[...]
    y = x * lax.rsqrt(jnp.mean(x * x, axis=-1, keepdims=True) + EPS)
    o_ref[...] = (y * g_ref[...]).astype(o_ref.dtype)


def rmsnorm(x, g, out_dtype, tm=512):
    n, d = x.shape
    tm = min(tm, n)
    return pl.pallas_call(
        _rmsnorm_kernel,
        grid=(n // tm,),
        in_specs=[pl.BlockSpec((tm, d), lambda i: (i, 0)),
                  pl.BlockSpec((1, d), lambda i: (0, 0))],
        out_specs=pl.BlockSpec((tm, d), lambda i: (i, 0)),
        out_shape=jax.ShapeDtypeStruct((n, d), out_dtype),
        compiler_params=_params("parallel"),
        name="rmsnorm",
    )(x, g.reshape(1, d).astype(F32))


def _matmul_kernel(*refs, nk, act, act_from, has_norm, has_affine, has_res):
    refs = list(refs)
    x_ref = refs.pop(0)
    g_ref = refs.pop(0) if has_norm else None
    w_ref = refs.pop(0)
    s_ref, b_ref = (refs.pop(0), refs.pop(0)) if has_affine else (None, None)
    r_ref = refs.pop(0) if has_res else None
    o_ref = refs.pop(0)
    h_ref = refs.pop(0) if has_norm else None
    j = pl.program_id(1)

    if has_norm:
        @pl.when(j == 0)
        def _():
            x = x_ref[...]
            y = x * lax.rsqrt(jnp.mean(x * x, axis=-1, keepdims=True) + EPS)
            h_ref[...] = (y * g_ref[...]).astype(BF16)

        lhs_ref = h_ref
    else:
        lhs_ref = x_ref

    if nk > 1:
        @pl.when(pl.program_id(2) == 0)
        def _():
            o_ref[...] = r_ref[...] if has_res else jnp.zeros_like(o_ref)

    for c in range(o_ref.shape[1] // MXU_WIDTH):
        cols = slice(c * MXU_WIDTH, (c + 1) * MXU_WIDTH)
        acc = jnp.dot(lhs_ref[...], w_ref[:, cols].astype(BF16), preferred_element_type=F32)
        if nk > 1:
            o_ref[:, cols] += acc
            continue
        if has_affine:
            acc = acc * s_ref[:, cols] + b_ref[:, cols]
        if act is not None:
            activated = jax.nn.sigmoid(acc) if act == "sigmoid" else jnp.square(jnp.maximum(acc, 0.0))
            acc = activated if act_from == 0 else jnp.where(j >= act_from, activated, acc)
        if has_res:
            acc = acc + r_ref[:, cols]
        o_ref[:, cols] = acc.astype(o_ref.dtype)


def matmul(x, w, layer, *, norm_g=None, scale=None, bias=None, act=None, act_from_col=0, residual=None,
           out_dtype=BF16, tm=2048, tn=512, tk=2048):
    m, kdim = x.shape
    n = w.shape[2]
    tm, tn, tk = min(tm, m), min(tn, n), min(tk, kdim)
    nk = kdim // tk
    has_norm = norm_g is not None
    has_affine = scale is not None or bias is not None
    assert not has_norm or nk == 1, "the fused rmsnorm needs whole rows"
    assert nk == 1 or (out_dtype == F32 and act is None and not has_affine), "k-split accumulates in the output"
    assert act_from_col % tn == 0
    in_specs = [pl.BlockSpec((tm, tk), lambda i, j, k: (i, k))]
    args = [x]
    if has_norm:
        in_specs.append(pl.BlockSpec((1, kdim), lambda i, j, k: (0, 0)))
        args.append(norm_g.reshape(1, kdim).astype(F32))
    in_specs.append(pl.BlockSpec((None, tk, tn), lambda i, j, k: (layer, k, j)))
    args.append(w)
    if has_affine:
        row_spec = pl.BlockSpec((1, tn), lambda i, j, k: (0, j))
        in_specs += [row_spec, row_spec]
        args.append((jnp.ones((n,), F32) if scale is None else scale.astype(F32)).reshape(1, n))
        args.append((jnp.zeros((n,), F32) if bias is None else bias.astype(F32)).reshape(1, n))
    if residual is not None:
        in_specs.append(pl.BlockSpec((tm, tn), lambda i, j, k: (i, j)))
        args.append(residual)
    scratch = []
    if has_norm:
        scratch.append(pltpu.VMEM((tm, kdim), BF16))
    kern =functools.partial(_matmul_kernel, nk=nk, act=act, act_from=act_from_col // tn, has_norm=has_norm,
                             has_affine=has_affine, has_res=residual is not None)
    return pl.pallas_call(
        kern,
        grid=(m // tm, n // tn, nk),
        in_specs=in_specs,
        out_specs=pl.BlockSpec((tm, tn), lambda i, j, k: (i, j)),
        out_shape=jax.ShapeDtypeStruct((m, n), out_dtype),
        scratch_shapes=scratch,
        compiler_params=_params("parallel", "arbitrary", "arbitrary"),
        name="matmul",
    )(*args)


def _gmlp_kernel(uv_ref, g_ref, w_ref, bt_ref, o_ref, *, width):
    z = jax.nn.gelu(uv_ref[...].astype(F32))
    u = z[:, :width]
    v = z[:, width:]
    v = v * lax.rsqrt(jnp.mean(v * v, axis=-1, keepdims=True) + EPS) * g_ref[...]
    v = v.astype(BF16)
    tm = u.shape[0]
    t_idx = lax.broadcasted_iota(jnp.int32, (GM_CHUNK, GM_CHUNK), 0)
    s_idx = lax.broadcasted_iota(jnp.int32, (GM_CHUNK, GM_CHUNK), 1)
    causal = s_idx <= t_idx
    for g in range(width // GM_GROUP_CH):
        w = jnp.where(causal, w_ref[g], 0.0).astype(BF16)
        bias = bt_ref[:, g:g + 1]
        cols = slice(g * GM_GROUP_CH, (g + 1) * GM_GROUP_CH)
        for c in range(tm // GM_CHUNK):
            rows = slice(c * GM_CHUNK, (c + 1) * GM_CHUNK)
            mixed = jnp.dot(w, v[rows, cols], preferred_element_type=F32) + bias
            o_ref[rows, cols] = (u[rows, cols] * mixed).astype(o_ref.dtype)


def gmlp(proj, norm_g, w_s, b_s, *, width, tm=256):
    n = proj.shape[0]
    groups = width // GM_GROUP_CH
    return pl.pallas_call(
        functools.partial(_gmlp_kernel, width=width),
        grid=(n // tm,),
        in_specs=[pl.BlockSpec((tm, 2 * width), lambda i: (i, 0)),
                  pl.BlockSpec((1, width), lambda i: (0, 0)),
                  pl.BlockSpec((groups, GM_CHUNK, GM_CHUNK), lambda i: (0, 0, 0)),
                  pl.BlockSpec((GM_CHUNK, groups), lambda i: (0, 0))],
        out_specs=pl.BlockSpec((tm, width), lambda i: (i, 0)),
        out_shape=jax.ShapeDtypeStruct((n, width), BF16),
        compiler_params=_params("parallel"),
        name="gmlp",
    )(proj, norm_g.reshape(1, width).astype(F32), w_s.astype(F32), jnp.transpose(b_s).astype(F32))


SB_LOGIT_SCALE = SB_HEAD_DIM ** -0.5 * math.log2(math.e)


SB_MASKED_LOGIT = -1e30


def _sb_attn_kernel(q_ref, k_ref, v_ref, tri_ref, o_ref, z_scr, p_scr, tot_scr, acc_scr, *, tq, blk, hps):
    qi = pl.program_id(2)
    dh = SB_HEAD_DIM
    tri = tri_ref[...]
    nblk = tq // blk

    def logits(h, start, c, diagonal):
        hc = slice(h * dh, (h + 1) * dh)
        k = k_ref[pl.ds(pl.multiple_of(start + c * blk, blk), blk), hc]
        z = lax.dot_general(q_ref[:, hc], k, (((1,), (1,)), ((), ())), preferred_element_type=F32)
        if diagonal:
            t_idx = lax.broadcasted_iota(jnp.int32, (tq, blk), 0)
            s_idx = c * blk + lax.broadcasted_iota(jnp.int32, (tq, blk), 1)
            z = jnp.where(s_idx < t_idx, z, SB_MASKED_LOGIT)
        neg_abs = pltpu.bitcast(pltpu.bitcast(z, jnp.uint32) | jnp.uint32(0x80000000), F32)
        p = jnp.maximum(z, 0.0) + jnp.log2(1.0 + jnp.exp2(neg_abs))
        return z - p, p.astype(BF16), jnp.sum(p, axis=1, keepdims=True)

    def stash(h, c, vals):
        z_scr[h, c], p_scr[h, c], tot_scr[h, c] = vals

    def weights_step(h, start, c, run):
        hc = slice(h * dh, (h + 1) * dh)
        cs = jnp.dot(p_scr[h, c], tri, preferred_element_type=F32)
        w = jnp.exp2(z_scr[h, c] - cs - run).astype(BF16)
        v = v_ref[pl.ds(pl.multiple_of(start + c * blk, blk), blk), hc]
        acc_scr[h] = acc_scr[h] + jnp.dot(w, v, preferred_element_type=F32)
        return run + tot_scr[h, c]

    acc_scr[...] = jnp.zeros_like(acc_scr)
    for h in range(hps):
        for c in reversed(range(nblk)):
            stash(h, c, logits(h, qi * tq, c, True))

    def trip(i, carry):
        cur = (qi - i) * tq
        out = []
        for h in range(hps):
            run = carry[h]
            for c in reversed(range(nblk)):
                run = weights_step(h, cur, c, run)
                stash(h, c, logits(h, cur - tq, c, False))
            out.append(run)
        return tuple(out)

    carry = lax.fori_loop(0, qi, trip, tuple(jnp.zeros((tq, 1), F32) for _ in range(hps)))
    for h in range(hps):
        run = carry[h]
        for c in reversed(range(nblk)):
            run = weights_step(h, 0, c, run)
        o_ref[:, h * dh:(h + 1) * dh] = acc_scr[h].astype(o_ref.dtype)


def sb_attention(proj, q_col, k_col, v_col, *, heads, tq=512, blk=256, hps=2):
    bsz, seq, _ = proj.shape
    hw = hps * SB_HEAD_DIM
    tq = min(tq, seq)
    j_idx = jnp.arange(blk)
    tri = (j_idx[:, None] > j_idx[None, :]).astype(BF16)
    kern = functools.partial(_sb_attn_kernel, tq=tq, blk=blk, hps=hps)
    return pl.pallas_call(
        kern,
        grid=(bsz, heads // hps, seq // tq),
        in_specs=[pl.BlockSpec((None, tq, hw), lambda b, h, i: (b, i, q_col // hw + h)),
                  pl.BlockSpec((None, seq, hw), lambda b, h, i: (b, 0, k_col // hw + h)),
                  pl.BlockSpec((None, seq, hw), lambda b, h, i: (b, 0, v_col // hw + h)),
                  pl.BlockSpec((blk, blk), lambda b, h, i: (0, 0))],
        out_specs=pl.BlockSpec((None, tq, hw), lambda b, h, i: (b, i, h)),
        out_shape=jax.ShapeDtypeStruct((bsz, seq, heads * SB_HEAD_DIM), BF16),
        scratch_shapes=[pltpu.VMEM((hps, tq // blk, tq, blk), F32),
                        pltpu.VMEM((hps, tq // blk, tq, blk), BF16),
                        pltpu.VMEM((hps, tq // blk, tq, 1), F32),
                        pltpu.VMEM((hps, tq, SB_HEAD_DIM), F32)],
        compiler_params=_params("parallel", "parallel", "arbitrary"),
        name="sb_attention",
    )(proj, proj, proj, tri)


def _s5_kernel(x_ref, bmat_ref, cmat_ref, a_ref, at_ref, d_ref, wglu_ref, bglu_ref, o_ref,
               bu_ref, g_ref, fin_ref, sin_ref, carry_ref, *, tsub):
    nb = x_ref.shape[0]
    ns = S5_BUNDLE_STATE
    tc = tsub * S5_SEGMENTS

    @pl.when(pl.program_id(1) == 0)
    def _():
        carry_ref[...] = jnp.zeros_like(carry_ref)

    def cmul_add(ar, ai, sr, si, br, bi):
        return ar * sr - ai * si + br, ar * si + ai * sr + bi

    def b_matmul(j):
        bu_ref[j % 2] = jnp.dot(x_ref[j], bmat_ref[j], preferred_element_type=F32)

    def scan(j):
        slot = j % 2
        a = a_ref[j]
        ar = jnp.broadcast_to(a[:, :ns], (S5_SEGMENTS, ns))
        ai = jnp.broadcast_to(a[:, ns:], (S5_SEGMENTS, ns))

        def rows(i):
            return slice(i * S5_SEGMENTS, (i + 1) * S5_SEGMENTS)

        sr = si = jnp.zeros((S5_SEGMENTS, ns), F32)
        for i in range(tsub):
            sr, si = cmul_add(ar, ai, sr, si, bu_ref[slot, rows(i), :ns], bu_ref[slot, rows(i), ns:])
        fin_ref[j, :, :ns] = sr
        fin_ref[j, :, ns:] = si
        at = at_ref[j]
        atr, ati = at[:, :ns], at[:, ns:]
        pr = carry_ref[j, :, :ns]
        pi = carry_ref[j, :, ns:]
        for k in range(S5_SEGMENTS):
            sin_ref[j, k:k + 1, :ns] = pr
            sin_ref[j, k:k + 1, ns:] = pi
            pr, pi = cmul_add(atr, ati, pr, pi, fin_ref[j, k:k + 1, :ns], fin_ref[j, k:k + 1, ns:])
        carry_ref[j, :, :ns] = pr
        carry_ref[j, :, ns:] = pi
        sr, si = sin_ref[j, :, :ns], sin_ref[j, :, ns:]
        for i in range(tsub):
            sr, si = cmul_add(ar, ai, sr, si, bu_ref[slot, rows(i), :ns], bu_ref[slot, rows(i), ns:])
            bu_ref[slot, rows(i), :ns] = sr
            bu_ref[slot, rows(i), ns:] = si

    def c_matmul(j):
        y = jnp.dot(bu_ref[j % 2].astype(BF16), cmat_ref[j], preferred_element_type=F32)
        y = y + d_ref[j] * x_ref[j].astype(F32)
        g_ref[j] = jax.nn.gelu(y)

    b_matmul(0)
    for j in range(nb):
        if j + 1 < nb:
            b_matmul(j + 1)
        scan(j)
        c_matmul(j)

    gate = jnp.zeros((tc, nb * S5_BUNDLE_CH), F32)
    for j in range(nb):
        gate = gate + jnp.dot(g_ref[j].astype(BF16), wglu_ref[j], preferred_element_type=F32)
    gate = jax.nn.sigmoid(gate + bglu_ref[...])
    for j in range(nb):
        cols = slice(j * S5_BUNDLE_CH, (j + 1) * S5_BUNDLE_CH)
        o_ref[:, cols] = (g_ref[j] * gate[:, cols]).astype(o_ref.dtype)


def _s5_tables(lam_re, lam_im, log_dt, b_re, b_im, c_re, c_im, tsub):
    groups = lam_re.shape[0]
    nb = groups // S5_BUNDLE_GROUPS
    dt = jnp.exp(log_dt.astype(F32))[:, None]
    lam_re = lam_re.astype(F32)
    lam_im = lam_im.astype(F32)
    mag = jnp.exp(lam_re * dt)
    ab_re = mag * jnp.cos(lam_im * dt)
    ab_im = mag * jnp.sin(lam_im * dt)
    den = lam_re * lam_re + lam_im * lam_im
    n_re = ab_re - 1.0
    n_im = ab_im
    k_re = (n_re * lam_re + n_im * lam_im) / den
    k_im = (n_im * lam_re - n_re * lam_im) / den
    b_re = b_re.astype(F32)
    b_im = b_im.astype(F32)
    bb_re = k_re[..., None] * b_re - k_im[..., None] * b_im
    bb_im = k_re[..., None] * b_im + k_im[..., None] * b_re
    magt = jnp.exp(lam_re * dt * tsub)
    at_re = magt * jnp.cos(lam_im * dt * tsub)
    at_im = magt * jnp.sin(lam_im * dt * tsub)
    eye = jnp.eye(S5_BUNDLE_GROUPS, dtype=F32)

    def pack_b(bb):
        bb = bb.reshape(nb, S5_BUNDLE_GROUPS, S5_STATE, S5_GROUP_CH)
        return jnp.einsum("jgph,gk->jghkp", bb, eye).reshape(nb, S5_BUNDLE_CH, S5_BUNDLE_STATE)

    def pack_c(c):
        c = c.astype(F32).reshape(nb, S5_BUNDLE_GROUPS, S5_GROUP_CH, S5_STATE)
        return jnp.einsum("jghp,gk->jkpgh", c, eye).reshape(nb, S5_BUNDLE_STATE, S5_BUNDLE_CH)

    def pack_a(re, im):
        return jnp.concatenate([re.reshape(nb, 1, S5_BUNDLE_STATE), im.reshape(nb, 1, S5_BUNDLE_STATE)], axis=-1)

    bmat = jnp.concatenate([pack_b(bb_re), pack_b(bb_im)], axis=-1).astype(BF16)
    cmat = jnp.concatenate([pack_c(c_re), -pack_c(c_im)], axis=1).astype(BF16)
    return bmat, cmat, pack_a(ab_re, ab_im), pack_a(at_re, at_im)


def s5_mixer(xin, lam_re, lam_im, log_dt, b_re, b_im, c_re, c_im, d, w_glu, b_glu, *, tc=512):
    bsz, seq, width = xin.shape
    tc = min(tc, seq)
    tsub = tc // S5_SEGMENTS
    nb = width // S5_BUNDLE_CH
    nchunk = seq // tc
    bmat, cmat, a, at = _s5_tables(lam_re, lam_im, log_dt, b_re, b_im, c_re, c_im, tsub)
    xp = xin.reshape(bsz, nchunk, S5_SEGMENTS, tsub, nb, S5_BUNDLE_CH)
    xp = jnp.transpose(xp, (0, 4, 1, 3, 2, 5)).reshape(bsz, nb, seq, S5_BUNDLE_CH)
    ns2 = 2 * S5_BUNDLE_STATE
    out = pl.pallas_call(
        functools.partial(_s5_kernel, tsub=tsub),
        grid=(bsz, nchunk),
        in_specs=[pl.BlockSpec((None, nb, tc, S5_BUNDLE_CH), lambda b, c: (b, 0, c, 0)),
                  pl.BlockSpec((nb, S5_BUNDLE_CH, ns2), lambda b, c: (0, 0, 0)),
                  pl.BlockSpec((nb, ns2, S5_BUNDLE_CH), lambda b, c: (0, 0, 0)),
                  pl.BlockSpec((nb, 1, ns2), lambda b, c: (0, 0, 0)),
                  pl.BlockSpec((nb, 1, ns2), lambda b, c: (0, 0, 0)),
                  pl.BlockSpec((nb, 1, S5_BUNDLE_CH), lambda b, c: (0, 0, 0)),
                  pl.BlockSpec((nb, S5_BUNDLE_CH, width), lambda b, c: (0, 0, 0)),
                  pl.BlockSpec((1, width), lambda b, c: (0, 0))],
        out_specs=pl.BlockSpec((None, tc, width), lambda b, c: (b, c, 0)),
        out_shape=jax.ShapeDtypeStruct((bsz, seq, width), BF16),
        scratch_shapes=[pltpu.VMEM((2, tc, ns2), F32),
                        pltpu.VMEM((nb, tc, S5_BUNDLE_CH), F32),
                        pltpu.VMEM((nb, S5_SEGMENTS, ns2), F32),
                        pltpu.VMEM((nb, S5_SEGMENTS, ns2), F32),
                        pltpu.VMEM((nb, 1, ns2), F32)],
        compiler_params=_params("parallel", "arbitrary"),
        name="s5_mixer",
    )(xp, bmat, cmat, a, at, d.reshape(nb, 1, S5_BUNDLE_CH).astype(F32),
      w_glu.astype(BF16).reshape(nb, S5_BUNDLE_CH, width), b_glu.reshape(1, width).astype(F32))
    out = out.reshape(bsz, nchunk, tsub, S5_SEGMENTS, width)
    return jnp.transpose(out, (0, 1, 3, 2, 4)).reshape(bsz, seq, width)


def _merge_kernel(ya_ref, yb_ref, yc_ref, ga_ref, gb_ref, gc_ref, w_ref, o_ref):
    for c in range(o_ref.shape[1] // MXU_WIDTH):
        cols = slice(c * MXU_WIDTH, (c + 1) * MXU_WIDTH)
        acc = jnp.zeros((o_ref.shape[0], MXU_WIDTH), F32)
        for n, (y_ref, g_ref) in enumerate(((ya_ref, ga_ref), (yb_ref, gb_ref), (yc_ref, gc_ref))):
            gate = jax.nn.sigmoid(g_ref[:, cols].astype(F32))
            acc += gate * jnp.dot(y_ref[...], w_ref[n, :, cols].astype(BF16), preferred_element_type=F32)
        o_ref[:, cols] = acc.astype(o_ref.dtype)


def merge_branches(ya, yb, yc, gates, gate_col, w_branch, layer, *, tm=512):
    n, width = ya.shape
    d = w_branch.shape[3]
    assert gate_col % d == 0
    y_spec = pl.BlockSpec((tm, width), lambda i: (i, 0))

    def gate_spec(branch):
        return pl.BlockSpec((tm, d), lambda i: (i, gate_col // d + branch))

    return pl.pallas_call(
        _merge_kernel,
        grid=(n // tm,),
        in_specs=[y_spec, y_spec, y_spec, gate_spec(0), gate_spec(1), gate_spec(2),
                  pl.BlockSpec((None, 3, width, d), lambda i: (layer, 0, 0, 0), pipeline_mode=pl.Buffered(1))],
        out_specs=pl.BlockSpec((tm, d), lambda i: (i, 0)),
        out_shape=jax.ShapeDtypeStruct((n, d), BF16),
        compiler_params=_params("parallel"),
        name="merge_branches",
    )(ya, yb, yc, gates, gates, gates, w_branch)


def kernel(x, norm1_g, w_in, b_gate, gm_norm_g, gm_w_s, gm_b_s, s5_lambda_re, s5_lambda_im, s5_log_dt,
           s5_b_re, s5_b_im, s5_c_re, s5_c_im, s5_d, s5_w_glu, s5_b_glu, w_branch, w_out, norm2_g,
           w_mlp_in, w_mlp_out, final_g):
    bsz, seq, d_model = x.shape
    depth = w_in.shape[0]
    width = gm_norm_g.shape[1]
    heads = width // SB_HEAD_DIM
    o_a = 2 * width
    o_b = o_a + width
    o_c = o_b + 3 * width
    n = bsz * seq
    xf = x.reshape(n, d_model)
    col = jnp.arange(w_in.shape[2])
    col_scale = jnp.where((col >= o_b) & (col < o_b + width), SB_LOGIT_SCALE, 1.0).astype(F32)
    for l in range(depth):
        gate_bias = jnp.concatenate([jnp.zeros((o_c,), F32), b_gate[l].reshape(-1).astype(F32)])
        proj = matmul(xf, w_in, l, norm_g=norm1_g[l], scale=col_scale, bias=gate_bias)
        y_a = gmlp(proj, gm_norm_g[l], gm_w_s[l], gm_b_s[l], width=width)
        proj3 = proj.reshape(bsz, seq, proj.shape[1])
        y_b = s5_mixer(proj3[:, :, o_a:o_b], s5_lambda_re[l], s5_lambda_im[l], s5_log_dt[l],
                       s5_b_re[l], s5_b_im[l], s5_c_re[l], s5_c_im[l], s5_d[l], s5_w_glu[l], s5_b_glu[l])
        y_c = sb_attention(proj3, o_b, o_b + width, o_b + 2 * width, heads=heads)
        merged = merge_branches(y_a, y_b.reshape(n, width), y_c.reshape(n, width), proj, o_c, w_branch, l)
        xf = matmul(merged, w_out, l, residual=xf, out_dtype=F32)
        act = matmul(xf, w_mlp_in, l, norm_g=norm2_g[l], act="relu2")
        xf = matmul(act, w_mlp_out, l, residual=xf, out_dtype=F32, tn=1024, tk=1024)
    return rmsnorm(xf, final_g, x.dtype).reshape(bsz, seq, d_model)
```
